```python
import math
import jax
import jax.numpy as jnp
from jax import lax
import numpy as np

D_MODEL = 2048
BATCH = 32
SEQ = 256
DEPTH = 2
DEC_BATCH = 4
DEC_SEQ = 2048
PAST_LEN = 512

GRID_W = 64
BRANCH_WIDTH = 1024
SSD_INNER = BRANCH_WIDTH
SSD_HEADDIM = 64
SSD_HEADS = SSD_INNER // SSD_HEADDIM
SSD_GROUPS = 2
SSD_STATE = 64
SSD_CONV = 7
SSD_CHUNK = 128
SSD_BC = SSD_GROUPS * SSD_STATE
DA_HEADS = 8
DA_HEADDIM = 64
DA_WIDTH = DA_HEADS * 2 * DA_HEADDIM
ROPE_BASE = 10000.0
Q_BLOCK = 128
SG_WIDTH = BRANCH_WIDTH
SG_CHUNK = 128
SG_GROUPS = 8
SG_GROUP_CH = SG_WIDTH // SG_GROUPS
N_BRANCH = 3
N_EXPERTS = 32
TOP_K = 4
D_EXPERT = D_MODEL
SWIGLU_LIMIT = 7.0
SWIGLU_ALPHA = 1.702
MOE_BLOCK = 128
NORM_EPS = 1e-6

IN_SIZES = (SSD_INNER, SSD_INNER, SSD_BC, SSD_BC, 2 * SSD_HEADS,
            DA_WIDTH, DA_WIDTH, DA_WIDTH,
            SG_WIDTH, SG_WIDTH,
            N_BRANCH * D_MODEL)
IN_SPLITS = tuple(int(s) for s in np.cumsum(IN_SIZES)[:-1])
IN_WIDTH = int(sum(IN_SIZES))

kernel_name = 'hybrid_flow_ssd_diffattn_sgu_moe_step'

F32 = jnp.float32


def rms_norm(x, g):
    xf = x.astype(F32)
    y = xf * lax.rsqrt(jnp.mean(xf * xf, axis=-1, keepdims=True) + NORM_EPS)
    return (y * g.astype(F32)).astype(x.dtype)


def layer_norm(x, g):
    xf = x.astype(F32)
    mu = jnp.mean(xf, axis=-1, keepdims=True)
    var = jnp.mean(jnp.square(xf - mu), axis=-1, keepdims=True)
    return ((xf - mu) * lax.rsqrt(var + NORM_EPS) * g.astype(F32)).astype(x.dtype)


def axial_rope_tables(n_tokens):
    rows = n_tokens // GRID_W
    t = jnp.arange(rows * GRID_W)
    row = (t // GRID_W).astype(F32)
    col = (t % GRID_W).astype(F32)
    quarter = DA_HEADDIM // 4
    inv_freq = ROPE_BASE ** (-jnp.arange(quarter, dtype=F32) / quarter)
    ang_r = row[:, None] * inv_freq
    ang_c = col[:, None] * inv_freq
    ang = jnp.concatenate([ang_r, ang_r, ang_c, ang_c], axis=-1)
    return jnp.cos(ang), jnp.sin(ang)


def apply_axial_rope(x, cos, sin):
    a, b, c, d = jnp.split(x, 4, axis=-1)
    rot = jnp.concatenate([-b, a, -d, c], axis=-1)
    cs = cos[None, :, None, None, :]
    sn = sin[None, :, None, None, :]
    return (x.astype(F32) * cs + rot.astype(F32) * sn).astype(x.dtype)


def depthwise_conv(x, w, b):
    ch = x.shape[-1]
    pad = (SSD_CONV - 1) // 2
    y = lax.conv_general_dilated(x, w[:, None, :].astype(x.dtype), window_strides=(1,),
                                 padding=[(pad, pad)], dimension_numbers=('NWC', 'WIO', 'NWC'),
                                 feature_group_count=ch)
    return y + b.astype(x.dtype)


def ssd_chunked(x, dt, a, bm, cm, h0):
    bsz, n, nh, hp = x.shape
    q = SSD_CHUNK
    nc = n // q
    rep = nh // bm.shape[2]
    bh = jnp.repeat(bm, rep, axis=2).astype(F32).reshape(bsz, nc, q, nh, SSD_STATE)
    ch = jnp.repeat(cm, rep, axis=2).astype(F32).reshape(bsz, nc, q, nh, SSD_STATE)
    dtc = dt.reshape(bsz, nc, q, nh)
    xdt = x.astype(F32).reshape(bsz, nc, q, nh, hp) * dtc[..., None]
    acum = jnp.cumsum(dtc * a, axis=2)
    at = jnp.swapaxes(acum, 2, 3)
    idx = jnp.arange(q)
    lower = idx[:, None] >= idx[None, :]
    decay = jnp.exp(jnp.where(lower, at[..., :, None] - at[..., None, :], -jnp.inf))
    scores = jnp.einsum('bcihn,bcjhn->bchij', ch, bh) * decay
    y_diag = jnp.einsum('bchij,bcjhp->bcihp', scores, xdt)
    decay_end = jnp.exp(acum[:, :, -1:, :] - acum)
    states = jnp.einsum('bcjhn,bcjh,bcjhp->bchpn', bh, decay_end, xdt)
    chunk_decay = jnp.exp(acum[:, :, -1, :])

    def step(h, inp):
        s_c, d_c = inp
        return d_c[:, :, None, None] * h + s_c, h

    h_final, h_enter = lax.scan(step, h0, (jnp.moveaxis(states, 1, 0), jnp.moveaxis(chunk_decay, 1, 0)))
    h_enter = jnp.moveaxis(h_enter, 0, 1)
    y_off = jnp.einsum('bcihn,bchpn,bcih->bcihp', ch, h_enter, jnp.exp(acum))
    return (y_diag + y_off).reshape(bsz, n, nh, hp), h_final


def ssd_branch(z, xbc, dt_raw, conv_w, conv_b, dt_bias, a_log, d_skip, norm_g, ctx_state):
    bsz, n = z.shape[:2]
    xbc = jax.nn.silu(depthwise_conv(xbc, conv_w, conv_b))
    xs, bm, cm = jnp.split(xbc, (SSD_INNER, SSD_INNER + SSD_BC), axis=-1)
    xs = xs.reshape(bsz, n, SSD_HEADS, SSD_HEADDIM)
    bm = bm.reshape(bsz, n, SSD_GROUPS, SSD_STATE)
    cm = cm.reshape(bsz, n, SSD_GROUPS, SSD_STATE)
    dt = jax.nn.softplus(dt_raw.astype(F32).reshape(bsz, n, 2, SSD_HEADS) + dt_bias.astype(F32))
    a = -jnp.exp(a_log.astype(F32))
    if ctx_state is None:
        h0f = jnp.zeros((bsz, SSD_HEADS, SSD_HEADDIM, SSD_STATE), F32)
        h0b = h0f
    else:
        h0f = ctx_state[:, 0].astype(F32)
        h0b = ctx_state[:, 1].astype(F32)
    flip = lambda t: jnp.flip(t, axis=1)
    y_f, h_f = ssd_chunked(xs, dt[:, :, 0], a[0], bm, cm, h0f)
    y_b, h_b = ssd_chunked(flip(xs), flip(dt[:, :, 1]), a[1], flip(bm), flip(cm), h0b)
    y = y_f + flip(y_b) + d_skip.astype(F32)[:, None] * xs.astype(F32)
    y = y.reshape(bsz, n, SSD_INNER).astype(z.dtype) * jax.nn.silu(z)
    return rms_norm(y, norm_g), jnp.stack([h_f, h_b], axis=1)


def block_diff_attention(q, k, v, lam):
    bsz, nq_tok = q.shape[:2]
    nq = nq_tok // Q_BLOCK
    qb = jnp.swapaxes(q.reshape(bsz, nq, Q_BLOCK, DA_HEADS, 2, DA_HEADDIM), 0, 1)
    scale = DA_HEADDIM ** -0.5

    def one_block(qblk):
        s = jnp.einsum('bqhmd,bkhmd->bhmqk', qblk, k, preferred_element_type=F32) * scale
        p = jax.nn.softmax(s, axis=-1)
        w = (p[:, :, 0] - lam * p[:, :, 1]).astype(v.dtype)
        return jnp.einsum('bhqk,bkhe->bqhe', w, v)

    o = lax.map(one_block, qb)
    return jnp.swapaxes(o, 0, 1).reshape(bsz, nq_tok, DA_HEADS, 2 * DA_HEADDIM)


def diff_attn_branch(q, k, v, qk_g, lam_vec, subln_g, layer, ctx_k, ctx_v):
    bsz, n = q.shape[:2]
    q = rms_norm(q.reshape(bsz, n, DA_HEADS, 2, DA_HEADDIM), qk_g[0])
    k = rms_norm(k.reshape(bsz, n, DA_HEADS, 2, DA_HEADDIM), qk_g[1])
    v = v.reshape(bsz, n, DA_HEADS, 2 * DA_HEADDIM)
    k_store = k.reshape(bsz, n, DA_HEADS, 2 * DA_HEADDIM)
    if ctx_k is None:
        k_all, v_all = k, v
    else:
        cos, sin = axial_rope_tables(n)
        q = apply_axial_rope(q, cos, sin)
        k = apply_axial_rope(k, cos, sin)
        lc = ctx_k.shape[1]
        k_all = jnp.concatenate([k, ctx_k.astype(k.dtype).reshape(bsz, lc, DA_HEADS, 2, DA_HEADDIM)], axis=1)
        v_all = jnp.concatenate([v, ctx_v.astype(v.dtype)], axis=1)
    lam_init = 0.8 - 0.6 * math.exp(-0.3 * layer)
    lv = lam_vec.astype(F32)
    lam = jnp.exp(jnp.sum(lv[0] * lv[1])) - jnp.exp(jnp.sum(lv[2] * lv[3])) + lam_init
    o = block_diff_attention(q, k_all, v_all, lam)
    o = rms_norm(o, subln_g) * (1.0 - lam_init)
    return o.reshape(bsz, n, DA_WIDTH), k_store, v


def sgu_branch(u, v, norm_g, w_s, b_s):
    bsz, n = v.shape[:2]
    nc = n // SG_CHUNK
    vn = layer_norm(v, norm_g).reshape(bsz, nc, SG_CHUNK, SG_GROUPS, SG_GROUP_CH)
    mixed = jnp.einsum('gij,bcjgd->bcigd', w_s.astype(vn.dtype), vn) + jnp.swapaxes(b_s, 0, 1).astype(vn.dtype)[None, None, :, :, None]
    return u * mixed.reshape(bsz, n, SG_WIDTH)


def token_mixer(h, p, l, ctx_k, ctx_v, ctx_state):
    bsz, n, _ = h.shape
    proj = jnp.einsum('bld,de->ble', h, p['w_in'][l])
    z, xs, bm, cm, dt_raw, q, k, v, su, sv, gates = jnp.split(proj, IN_SPLITS, axis=-1)
    y_a, ssm_state = ssd_branch(z, jnp.concatenate([xs, bm, cm], axis=-1), dt_raw,
                                p['conv_w'][l], p['conv_b'][l], p['dt_bias'][l], p['a_log'][l],
                                p['d_skip'][l], p['ssm_norm_g'][l], ctx_state)
    y_b, k_store, v_store = diff_attn_branch(q, k, v, p['qk_norm_g'][l], p['lam'][l], p['subln_g'][l],
                                             l, ctx_k, ctx_v)
    y_c = sgu_branch(su, sv, p['sgu_norm_g'][l], p['w_s'][l], p['b_s'][l])
    ys = jnp.stack([y_a, y_b, y_c], axis=2)
    g = jax.nn.sigmoid(gates.astype(F32)).reshape(bsz, n, N_BRANCH, D_MODEL).astype(h.dtype)
    merged = jnp.sum(jnp.einsum('blkw,kwd->blkd', ys, p['w_branch'][l]) * g, axis=2)
    out = jnp.einsum('bld,de->ble', merged, p['w_out'][l])
    return out, k_store, v_store, ssm_state


def moe(h, w_router, b_router, w_gu, b_gu, w_down, b_down):
    bsz, n, d = h.shape
    t = bsz * n
    hf = h.reshape(t, d)
    logits = (hf @ w_router + b_router).astype(F32)
    top_v, top_i = lax.top_k(logits, TOP_K)
    top_w = jax.nn.softmax(top_v, axis=-1)
    n_assign = t * TOP_K
    e_flat = top_i.reshape(n_assign)
    tok_flat = jnp.repeat(jnp.arange(t, dtype=jnp.int32), TOP_K)
    w_flat = top_w.reshape(n_assign)
    order = jnp.argsort(e_flat)
    e_sorted = e_flat[order]
    tok_sorted = tok_flat[order]
    w_sorted = w_flat[order]
    counts = jnp.bincount(e_flat, length=N_EXPERTS)
    starts = jnp.cumsum(counts) - counts
    padded = (counts + MOE_BLOCK - 1) // MOE_BLOCK * MOE_BLOCK
    pad_end = jnp.cumsum(padded)
    pad_start = pad_end - padded
    dest = pad_start[e_sorted] + jnp.arange(n_assign) - starts[e_sorted]
    n_blocks = -(-n_assign // MOE_BLOCK) + N_EXPERTS
    n_rows = n_blocks * MOE_BLOCK
    row_tok = jnp.full((n_rows,), t, jnp.int32).at[dest].set(tok_sorted)
    row_w = jnp.zeros((n_rows,), F32).at[dest].set(w_sorted)
    blk_expert = jnp.minimum(jnp.searchsorted(pad_end, jnp.arange(n_blocks) * MOE_BLOCK, side='right'),
                             N_EXPERTS - 1)
    x_rows = jnp.concatenate([hf, jnp.zeros((1, d), hf.dtype)], axis=0)[row_tok]
    x_rows = x_rows.reshape(n_blocks, MOE_BLOCK, d)

    def expert_block(args):
        xb, e = args
        gu = xb @ w_gu[e] + b_gu[e]
        gate, up = jnp.split(gu, 2, axis=-1)
        gate = jnp.minimum(gate, SWIGLU_LIMIT)
        up = jnp.clip(up, -SWIGLU_LIMIT, SWIGLU_LIMIT)
        act = (up + 1.0) * gate * jax.nn.sigmoid(SWIGLU_ALPHA * gate)
        return act @ w_down[e] + b_down[e]

    y_rows = lax.map(expert_block, (x_rows, blk_expert)).reshape(n_rows, d)
    y_rows = y_rows.astype(F32) * row_w[:, None]
    out = jax.ops.segment_sum(y_rows, row_tok, num_segments=t + 1)[:t]
    return out.astype(h.dtype).reshape(bsz, n, d)


def trunk_layer(x, cond, p, l, ctx_k, ctx_v, ctx_state):
    mod = jnp.dot(jax.nn.silu(cond), p['w_ada'][l]) + p['b_ada'][l]
    sh1, sc1, g1, sh2, sc2, g2 = jnp.split(mod, 6, axis=-1)
    h = rms_norm(x, p['norm1_g'][l]) * (1.0 + sc1[:, None]) + sh1[:, None]
    out, k_store, v_store, ssm_state = token_mixer(h, p, l, ctx_k, ctx_v, ctx_state)
    x = x + g1[:, None] * out
    h = rms_norm(x, p['norm2_g'][l]) * (1.0 + sc2[:, None]) + sh2[:, None]
    x = x + g2[:, None] * moe(h, p['w_router'][l], p['b_router'][l], p['w_gu'][l], p['b_gu'][l],
                              p['w_down'][l], p['b_down'][l])
    return x, k_store, v_store, ssm_state


def setup_inputs(seed: int = 0) -> dict:
    key = jax.random.key(seed)
    ks = jax.random.split(key, 40)
    nrm = lambda k, shape, s: jax.random.normal(k, shape, F32) * s
    gain = lambda k, shape: 1.0 + 0.01 * jax.random.normal(k, shape, F32)
    dt = jnp.exp(jax.random.uniform(ks[14], (DEPTH, 2, SSD_HEADS), F32) * (math.log(0.1) - math.log(1e-3))
                 + math.log(1e-3))
    return {
        'x_prompt': nrm(ks[0], (BATCH, SEQ, D_MODEL), 1.0),
        'x_sample': nrm(ks[1], (DEC_BATCH, DEC_SEQ, D_MODEL), 1.0),
        'c': nrm(ks[2], (DEC_BATCH, D_MODEL), 1.0),
        'cache_k': nrm(ks[3], (DEC_BATCH, DEPTH, PAST_LEN, DA_HEADS, 2 * DA_HEADDIM), 1.0),
        'cache_v': nrm(ks[4], (DEC_BATCH, DEPTH, PAST_LEN, DA_HEADS, 2 * DA_HEADDIM), 1.0),
        'state_ssm': nrm(ks[5], (DEC_BATCH, DEPTH, 2, SSD_HEADS, SSD_HEADDIM, SSD_STATE), 0.5),
        'c_ctx': nrm(ks[6], (D_MODEL,), 1.0),
        'w_ada': nrm(ks[7], (DEPTH, D_MODEL, 6 * D_MODEL), 0.5 * D_MODEL ** -0.5),
        'b_ada': nrm(ks[8], (DEPTH, 6 * D_MODEL), 0.01),
        'norm1_g': gain(ks[9], (DEPTH, D_MODEL)),
        'norm2_g': gain(ks[10], (DEPTH, D_MODEL)),
        'w_in': nrm(ks[11], (DEPTH, D_MODEL, IN_WIDTH), D_MODEL ** -0.5),
        'conv_w': nrm(ks[12], (DEPTH, SSD_CONV, SSD_INNER + 2 * SSD_BC), SSD_CONV ** -0.5),
        'conv_b': nrm(ks[13], (DEPTH, SSD_INNER + 2 * SSD_BC), 0.01),
        'dt_bias': dt + jnp.log(-jnp.expm1(-dt)),
        'a_log': jnp.log(jax.random.uniform(ks[15], (DEPTH, 2, SSD_HEADS), F32, minval=1.0, maxval=16.0)),
        'd_skip': gain(ks[16], (DEPTH, SSD_HEADS)),
        'ssm_norm_g': gain(ks[17], (DEPTH, SSD_INNER)),
        'qk_norm_g': gain(ks[18], (DEPTH, 2, DA_HEADDIM)),
        'lam': nrm(ks[19], (DEPTH, 4, DA_HEADDIM), 0.1),
        'subln_g': gain(ks[20], (DEPTH, 2 * DA_HEADDIM)),
        'sgu_norm_g': gain(ks[21], (DEPTH, SG_WIDTH)),
        'w_s': nrm(ks[22], (DEPTH, SG_GROUPS, SG_CHUNK, SG_CHUNK), SG_CHUNK ** -0.5),
        'b_s': gain(ks[23], (DEPTH, SG_GROUPS, SG_CHUNK)),
        'w_branch': nrm(ks[24], (DEPTH, N_BRANCH, BRANCH_WIDTH, D_MODEL), BRANCH_WIDTH ** -0.5),
        'w_out': nrm(ks[25], (DEPTH, D_MODEL, D_MODEL), D_MODEL ** -0.5),
        'w_router': nrm(ks[26], (DEPTH, D_MODEL, N_EXPERTS), D_MODEL ** -0.5),
        'b_router': nrm(ks[27], (DEPTH, N_EXPERTS), 0.01),
        'w_gu': nrm(ks[28], (DEPTH, N_EXPERTS, D_MODEL, 2 * D_EXPERT), D_MODEL ** -0.5),
        'b_gu': nrm(ks[29], (DEPTH, N_EXPERTS, 2 * D_EXPERT), 0.01),
        'w_down': nrm(ks[30], (DEPTH, N_EXPERTS, D_EXPERT, D_MODEL), D_EXPERT ** -0.5),
        'b_down': nrm(ks[31], (DEPTH, N_EXPERTS, D_MODEL), 0.01),
    }


def reference(x_prompt, x_sample, c, cache_k, cache_v, state_ssm, c_ctx, w_ada, b_ada, norm1_g, norm2_g,
              w_in, conv_w, conv_b, dt_bias, a_log, d_skip, ssm_norm_g, qk_norm_g, lam, subln_g,
              sgu_norm_g, w_s, b_s, w_branch, w_out, w_router, b_router, w_gu, b_gu, w_down, b_down):
    p = {'w_ada': w_ada, 'b_ada': b_ada, 'norm1_g': norm1_g, 'norm2_g': norm2_g, 'w_in': w_in,
         'conv_w': conv_w, 'conv_b': conv_b, 'dt_bias': dt_bias, 'a_log': a_log, 'd_skip': d_skip,
         'ssm_norm_g': ssm_norm_g, 'qk_norm_g': qk_norm_g, 'lam': lam, 'subln_g': subln_g,
         'sgu_norm_g': sgu_norm_g, 'w_s': w_s, 'b_s': b_s, 'w_branch': w_branch, 'w_out': w_out,
         'w_router': w_router, 'b_router': b_router, 'w_gu': w_gu, 'b_gu': b_gu,
         'w_down': w_down, 'b_down': b_down}
    xp = x_prompt
    k_list, v_list, s_list = [], [], []
    for l in range(DEPTH):
        xp, k_l, v_l, s_l = trunk_layer(xp, c_ctx[None, :], p, l, None, None, None)
        k_list.append(k_l)
        v_list.append(v_l)
        s_list.append(s_l)
    new_cache_k = jnp.stack(k_list, axis=1)
    new_cache_v = jnp.stack(v_list, axis=1)
    new_state_ssm = jnp.stack(s_list, axis=1)
    xs = x_sample
    for l in range(DEPTH):
        xs, _, _, _ = trunk_layer(xs, c, p, l, cache_k[:, l], cache_v[:, l], state_ssm[:, l])
    return (xp, xs, new_cache_k, new_cache_v, new_state_ssm)
```

```python
import functools
import math

import jax
import jax.numpy as jnp
import numpy as np
from jax import lax
from jax.experimental import pallas as pl
from jax.experimental.pallas import tpu as pltpu

F32 = jnp.float32
BF16 = jnp.bfloat16

D_MODEL = 2048
BATCH = 32
SEQ = 256
DEPTH = 2
DEC_BATCH = 4
DEC_SEQ = 2048
PAST_LEN = 512
GRID_W = 64
BRANCH_WIDTH = 1024
SSD_HEADDIM = 64
SSD_HEADS = 16
SSD_GROUPS = 2
SSD_STATE = 64
SSD_CONV = 7
SSD_CHUNK = 128
SSD_BC = 128
DA_HEADS = 8
DA_HEADDIM = 64
ROPE_BASE = 10000.0
SG_CHUNK = 128
SG_GROUPS = 8
N_EXPERTS = 32
TOP_K = 4
D_EXPERT = 2048
SWIGLU_LIMIT = 7.0
SWIGLU_ALPHA = 1.702
NORM_EPS = 1e-6

N_PROMPT = BATCH * SEQ
N_SAMPLE = DEC_BATCH * DEC_SEQ
N_TOK = N_PROMPT + N_SAMPLE
N_COND = 1 + DEC_BATCH

LANE = 128
SUBLANE = 8
VMEM_LIMIT = 56 * 1024 * 1024

MOE_BM = 256
MOE_BLOCKS = N_TOK * TOP_K // MOE_BM + N_EXPERTS
MOE_ROWS = MOE_BLOCKS * MOE_BM


def _cparams(sem):
    return pltpu.CompilerParams(dimension_semantics=sem, vmem_limit_bytes=VMEM_LIMIT)


def _cond_of_tile(i, tm):
    row = i * tm
    return jnp.where(row < N_PROMPT, 0, 1 + (row - N_PROMPT) // DEC_SEQ)


def _ada_kernel(c_ref, w_ref, b_ref, o_ref):
    c = c_ref[...]
    s = (c * jax.nn.sigmoid(c)).astype(BF16)
    o_ref[0] = jnp.dot(s, w_ref[0].astype(BF16), preferred_element_type=F32) + b_ref[0]


def _ada_all(cond8, w_ada, b_ada):
    tn = 1024
    n = 6 * D_MODEL
    return pl.pallas_call(
        _ada_kernel,
        grid=(DEPTH, n // tn),
        in_specs=[
            pl.BlockSpec((SUBLANE, D_MODEL), lambda l, j: (0, 0)),
            pl.BlockSpec((1, D_MODEL, tn), lambda l, j: (l, 0, j)),
            pl.BlockSpec((1, 1, tn), lambda l, j: (l, 0, j)),
        ],
        out_specs=pl.BlockSpec((1, SUBLANE, tn), lambda l, j: (l, 0, j)),
        out_shape=jax.ShapeDtypeStruct((DEPTH, SUBLANE, n), F32),
        compiler_params=_cparams(("arbitrary", "arbitrary")),
        name="ada",
    )(cond8, w_ada, b_ada.reshape(DEPTH, 1, n))


def _prologue_kernel(x_ref, g_ref, sc_ref, sh_ref, wdt_ref, h_ref, dt_ref):
    x = x_ref[...]
    ms = jnp.mean(x * x, axis=-1, keepdims=True)
    y = x * lax.rsqrt(ms + NORM_EPS) * g_ref[...]
    h = (y * (1.0 + sc_ref[0]) + sh_ref[0]).astype(BF16)
    h_ref[...] = h
    dt_ref[...] = jnp.dot(h, wdt_ref[...], preferred_element_type=F32)


def _prologue(x, g, sc, sh, wdt):
    tm = 512
    return pl.pallas_call(
        _prologue_kernel,
        grid=(N_TOK // tm,),
        in_specs=[
            pl.BlockSpec((tm, D_MODEL), lambda i: (i, 0)),
            pl.BlockSpec((1, D_MODEL), lambda i: (0, 0)),
            pl.BlockSpec((1, 1, D_MODEL), lambda i: (_cond_of_tile(i, tm), 0, 0)),
            pl.BlockSpec((1, 1, D_MODEL), lambda i: (_cond_of_tile(i, tm), 0, 0)),
            pl.BlockSpec((D_MODEL, LANE), lambda i: (0, 0)),
        ],
        out_specs=[
            pl.BlockSpec((tm, D_MODEL), lambda i: (i, 0)),
            pl.BlockSpec((tm, LANE), lambda i: (i, 0)),
        ],
        out_shape=[
            jax.ShapeDtypeStruct((N_TOK, D_MODEL), BF16),
            jax.ShapeDtypeStruct((N_TOK, LANE), F32),
        ],
        compiler_params=_cparams(("arbitrary",)),
        name="prologue",
    )(x, g.reshape(1, D_MODEL), sc.reshape(N_COND, 1, D_MODEL), sh.reshape(N_COND, 1, D_MODEL), wdt)


def _mm_kernel(a_ref, w_ref, o_ref):
    o_ref[...] = jnp.dot(a_ref[...], w_ref[...], preferred_element_type=F32).astype(o_ref.dtype)


def _matmul(a, w, tm, tn, out_dtype, name):
    m, k = a.shape
    n = w.shape[1]
    return pl.pallas_call(
        _mm_kernel,
        grid=(m // tm, n // tn),
        in_specs=[
            pl.BlockSpec((tm, k), lambda i, j: (i, 0)),
            pl.BlockSpec((k, tn), lambda i, j: (0, j)),
        ],
        out_specs=pl.BlockSpec((tm, tn), lambda i, j: (i, j)),
        out_shape=jax.ShapeDtypeStruct((m, n), out_dtype),
        compiler_params=_cparams(("arbitrary", "arbitrary")),
        name=name,
    )(a, w)


def _split3(x):
    hi = x.astype(BF16)
    r = x - hi.astype(F32)
    mid = r.astype(BF16)
    lo = (r - mid.astype(F32)).astype(BF16)
    return hi, mid, lo


def _ssd_kernel(fwd_ref, bwd_ref, first_ref, seq_ref, h0i_ref,
                xf_ref, xb_ref, dtf_ref, dtb_ref, bias_ref, a_ref, h0_ref,
                yf_ref, yb_ref, hout_ref, hstate):
    s = pl.program_id(0)
    q = SSD_CHUNK

    @pl.when(first_ref[s] == 1)
    def _():
        hstate[...] = h0_ref[0]

    ri = lax.broadcasted_iota(jnp.int32, (q, q), 0)
    ci = lax.broadcasted_iota(jnp.int32, (q, q), 1)

    for d in (0, 1):
        x_ref, dt_ref, y_ref = ((xf_ref, dtf_ref, yf_ref), (xb_ref, dtb_ref, yb_ref))[d]
        xbc = x_ref[...].astype(F32)
        dt = jax.nn.softplus(dt_ref[...] + bias_ref[...])
        adt = dt * a_ref[...]
        mask = (ri >= ci) if d == 0 else (ri <= ci)
        tri = jnp.where(mask, 1.0, 0.0).astype(BF16)
        hi, mid, lo = _split3(adt)
        cum = (jnp.dot(tri, hi, preferred_element_type=F32)
               + jnp.dot(tri, mid, preferred_element_type=F32)
               + jnp.dot(tri, lo, preferred_element_type=F32))
        cum_t = cum.T
        dt_t = dt.T
        tot = cum[q - 1:q, :] if d == 0 else cum[0:1, :]
        bm = xbc[:, 1024:1152]
        cm = xbc[:, 1152:1280]
        bm_t = bm.T
        ys = []
        for g in range(SSD_GROUPS):
            cg = cm[:, g * SSD_STATE:(g + 1) * SSD_STATE]
            bg = bm[:, g * SSD_STATE:(g + 1) * SSD_STATE]
            cb = lax.dot_general(cg, bg, (((1,), (1,)), ((), ())), preferred_element_type=F32)
            bg_t = bm_t[g * SSD_STATE:(g + 1) * SSD_STATE, :]
            for hh in range(SSD_HEADS // SSD_GROUPS):
                h = g * (SSD_HEADS // SSD_GROUPS) + hh
                ch = d * SSD_HEADS + h
                pcol = cum[:, ch:ch + 1]
                prow = cum_t[ch:ch + 1, :]
                dtrow = dt_t[ch:ch + 1, :]
                lm = jnp.exp(jnp.where(mask, pcol - prow, -1e30))
                scores = cb * lm * dtrow
                xs_h = xbc[:, h * SSD_HEADDIM:(h + 1) * SSD_HEADDIM]
                h_t = hstate[d, h]
                y = jnp.dot(scores, xs_h, preferred_element_type=F32)
                y = y + jnp.dot(cg, h_t, preferred_element_type=F32) * jnp.exp(pcol)
                tot_h = tot[:, ch:ch + 1]
                w_row = dtrow * jnp.exp(tot_h - prow)
                hstate[d, h] = jnp.exp(tot_h) * h_t + jnp.dot(bg_t * w_row, xs_h, preferred_element_type=F32)
                ys.append(y)
        y_ref[...] = jnp.concatenate(ys, axis=1)
    hout_ref[0] = hstate[...]


def _ssd_tables():
    fwd, bwd, first, seq, h0i = [], [], [], [], []
    cpr = SEQ // SSD_CHUNK
    for b in range(BATCH):
        for c in range(cpr):
            fwd.append(b * cpr + c)
            bwd.append(b * cpr + (cpr - 1 - c))
            first.append(1 if c == 0 else 0)
            seq.append(b)
            h0i.append(0)
    base = N_PROMPT // SSD_CHUNK
    cps = DEC_SEQ // SSD_CHUNK
    for b in range(DEC_BATCH):
        for c in range(cps):
            fwd.append(base + b * cps + c)
            bwd.append(base + b * cps + (cps - 1 - c))
            first.append(1 if c == 0 else 0)
            seq.append(BATCH + b)
            h0i.append(1 + b)
    return [jnp.asarray(np.asarray(t, np.int32)) for t in (fwd, bwd, first, seq, h0i)]


def _ssd(xc, dt_raw, dt_bias128, a128, h0_t):
    q = SSD_CHUNK
    nsteps = N_TOK // q
    wx = BRANCH_WIDTH + 2 * SSD_BC
    st_shape = (2, SSD_HEADS, SSD_STATE, SSD_HEADDIM)
    grid_spec = pltpu.PrefetchScalarGridSpec(
        num_scalar_prefetch=5,
        grid=(nsteps,),
        in_specs=[
            pl.BlockSpec((q, wx), lambda s, f, b, fi, sq, hi: (f[s], 0)),
            pl.BlockSpec((q, wx), lambda s, f, b, fi, sq, hi: (b[s], 0)),
            pl.BlockSpec((q, LANE), lambda s, f, b, fi, sq, hi: (f[s], 0)),
            pl.BlockSpec((q, LANE), lambda s, f, b, fi, sq, hi: (b[s], 0)),
            pl.BlockSpec((1, LANE), lambda s, f, b, fi, sq, hi: (0, 0)),
            pl.BlockSpec((1, LANE), lambda s, f, b, fi, sq, hi: (0, 0)),
            pl.BlockSpec((1,) + st_shape, lambda s, f, b, fi, sq, hi: (hi[s], 0, 0, 0, 0)),
        ],
        out_specs=[
            pl.BlockSpec((q, BRANCH_WIDTH), lambda s, f, b, fi, sq, hi: (f[s], 0)),
            pl.BlockSpec((q, BRANCH_WIDTH), lambda s, f, b, fi, sq, hi: (b[s], 0)),
            pl.BlockSpec((1,) + st_shape, lambda s, f, b, fi, sq, hi: (sq[s], 0, 0, 0, 0)),
        ],
        scratch_shapes=[pltpu.VMEM(st_shape, F32)],
    )
    return pl.pallas_call(
        _ssd_kernel,
        grid_spec=grid_spec,
        out_shape=[
            jax.ShapeDtypeStruct((N_TOK, BRANCH_WIDTH), F32),
            jax.ShapeDtypeStruct((N_TOK, BRANCH_WIDTH), F32),
            jax.ShapeDtypeStruct((BATCH + DEC_BATCH,) + st_shape, F32),
        ],
        compiler_params=_cparams(("arbitrary",)),
        name="ssd",
    )(*_ssd_tables(), xc, xc, dt_raw, dt_raw, dt_bias128, a128, h0_t)


def _attn_kernel(lam_ref, q_ref, k_ref, v_ref, g_ref, o_ref, *, lam_init):
    lam = lam_ref[0]
    qb = q_ref[...]
    kb = k_ref[...]
    scale = DA_HEADDIM ** -0.5
    es, inv = [], []
    for m in (0, 1):
        qm = qb[:, m * DA_HEADDIM:(m + 1) * DA_HEADDIM]
        km = kb[:, m * DA_HEADDIM:(m + 1) * DA_HEADDIM]
        s = lax.dot_general(qm, km, (((1,), (1,)), ((), ())), preferred_element_type=F32) * scale
        e = jnp.exp(s - jnp.max(s, axis=-1, keepdims=True))
        es.append(e)
        inv.append(1.0 / jnp.sum(e, axis=-1, keepdims=True))
    w = es[0] * inv[0] - es[1] * (lam * inv[1])
    o = jnp.dot(w.astype(BF16), v_ref[...], preferred_element_type=F32)
    o = o * lax.rsqrt(jnp.mean(o * o, axis=-1, keepdims=True) + NORM_EPS) * g_ref[...]
    o_ref[...] = (o * (1.0 - lam_init)).astype(o_ref.dtype)


def _attention(lam, q, k, v, subln_g, *, nseq, lq, lk, lam_init, name):
    tq = 256
    hw = 2 * DA_HEADDIM
    nqb = lq // tq
    grid_spec = pltpu.PrefetchScalarGridSpec(
        num_scalar_prefetch=0,
        grid=(nseq, DA_HEADS, nqb),
        in_specs=[
            pl.BlockSpec(memory_space=pltpu.SMEM),
            pl.BlockSpec((tq, hw), lambda b, h, i: (b * nqb + i, h)),
            pl.BlockSpec((lk, hw), lambda b, h, i: (b, h)),
            pl.BlockSpec((lk, hw), lambda b, h, i: (b, h)),
            pl.BlockSpec((1, hw), lambda b, h, i: (0, 0)),
        ],
        out_specs=pl.BlockSpec((tq, hw), lambda b, h, i: (b * nqb + i, h)),
    )
    return pl.pallas_call(
        functools.partial(_attn_kernel, lam_init=lam_init),
        grid_spec=grid_spec,
        out_shape=jax.ShapeDtypeStruct((nseq * lq, DA_HEADS * hw), BF16),
        compiler_params=_cparams(("arbitrary", "arbitrary", "arbitrary")),
        name=name,
    )(lam, q, k, v, subln_g.reshape(1, hw))


def _sgu_kernel(u_ref, v_ref, g_ref, ws_ref, bt_ref, o_ref, *, chunks):
    v = v_ref[...].astype(F32)
    mu = jnp.mean(v, axis=-1, keepdims=True)
    vc = v - mu
    var = jnp.mean(vc * vc, axis=-1, keepdims=True)
    vn = (vc * lax.rsqrt(var + NORM_EPS) * g_ref[...]).astype(BF16)
    u = u_ref[...].astype(F32)
    gw = BRANCH_WIDTH // SG_GROUPS
    for c in range(chunks):
        rows = slice(c * SG_CHUNK, (c + 1) * SG_CHUNK)
        outs = []
        for g in range(SG_GROUPS):
            cols = slice(g * gw, (g + 1) * gw)
            mixed = jnp.dot(ws_ref[g].astype(BF16), vn[rows, cols], preferred_element_type=F32)
            mixed = mixed + bt_ref[:, g:g + 1]
            outs.append(u[rows, cols] * mixed)
        o_ref[rows, :] = jnp.concatenate(outs, axis=1).astype(o_ref.dtype)


def _sgu(u, v, g, ws, bs_t):
    chunks = 4
    tm = chunks * SG_CHUNK
    return pl.pallas_call(
        functools.partial(_sgu_kernel, chunks=chunks),
        grid=(N_TOK // tm,),
        in_specs=[
            pl.BlockSpec((tm, BRANCH_WIDTH), lambda i: (i, 0)),
            pl.BlockSpec((tm, BRANCH_WIDTH), lambda i: (i, 1)),
            pl.BlockSpec((1, BRANCH_WIDTH), lambda i: (0, 0)),
            pl.BlockSpec((SG_GROUPS, SG_CHUNK, SG_CHUNK), lambda i: (0, 0, 0)),
            pl.BlockSpec((SG_CHUNK, SG_GROUPS), lambda i: (0, 0)),
        ],
        out_specs=pl.BlockSpec((tm, BRANCH_WIDTH), lambda i: (i, 0)),
        out_shape=jax.ShapeDtypeStruct((N_TOK, BRANCH_WIDTH), BF16),
        compiler_params=_cparams(("arbitrary",)),
        name="sgu",
    )(u, v, g.reshape(1, BRANCH_WIDTH), ws, bs_t)


def _merge_kernel(ya_ref, yb_ref, yc_ref, wb_ref, ga_ref, gb_ref, gc_ref, o_ref):
    acc = None
    for k, (y_ref, g_ref) in enumerate(((ya_ref, ga_ref), (yb_ref, gb_ref), (yc_ref, gc_ref))):
        t = jnp.dot(y_ref[...], wb_ref[k], preferred_element_type=F32)
        t = t * jax.nn.sigmoid(g_ref[...].astype(F32))
        acc = t if acc is None else acc + t
    o_ref[...] = acc.astype(o_ref.dtype)


def _merge(ya, yb, yc, wb, gates):
    tm, tn = 1024, 1024
    nj = D_MODEL // tn
    yspec = pl.BlockSpec((tm, BRANCH_WIDTH), lambda i, j: (i, 0))
    return pl.pallas_call(
        _merge_kernel,
        grid=(N_TOK // tm, nj),
        in_specs=[
            yspec, yspec, yspec,
            pl.BlockSpec((3, BRANCH_WIDTH, tn), lambda i, j: (0, 0, j)),
            pl.BlockSpec((tm, tn), lambda i, j: (i, j)),
            pl.BlockSpec((tm, tn), lambda i, j: (i, nj + j)),
            pl.BlockSpec((tm, tn), lambda i, j: (i, 2 * nj + j)),
        ],
        out_specs=pl.BlockSpec((tm, tn), lambda i, j: (i, j)),
        out_shape=jax.ShapeDtypeStruct((N_TOK, D_MODEL), BF16),
        compiler_params=_cparams(("arbitrary", "arbitrary")),
        name="merge",
    )(ya, yb, yc, wb, gates, gates, gates)


def _outproj_kernel(m_ref, w_ref, x_ref, g1_ref, n2_ref, sc_ref, sh_ref, wrh_ref, wrl_ref, br_ref,
                    x1_ref, h2_ref, lg_ref):
    out = jnp.dot(m_ref[...], w_ref[...], preferred_element_type=F32)
    x1 = x_ref[...] + g1_ref[0] * out
    x1_ref[...] = x1
    ms = jnp.mean(x1 * x1, axis=-1, keepdims=True)
    y = x1 * lax.rsqrt(ms + NORM_EPS) * n2_ref[...]
    h2 = y * (1.0 + sc_ref[0]) + sh_ref[0]
    hi = h2.astype(BF16)
    lo = (h2 - hi.astype(F32)).astype(BF16)
    h2_ref[...] = hi
    wh = wrh_ref[...]
    lg = (jnp.dot(hi, wh, preferred_element_type=F32) + jnp.dot(lo, wh, preferred_element_type=F32)
          + jnp.dot(hi, wrl_ref[...], preferred_element_type=F32))
    lg_ref[...] = lg + br_ref[...]


def _outproj(merged, w_out, x, g1, n2, sc2, sh2, wr_hi, wr_lo, br):
    tm = 512
    cond = lambda i: (_cond_of_tile(i, tm), 0, 0)
    row = lambda i: (i, 0)
    fixed = lambda i: (0, 0)
    return pl.pallas_call(
        _outproj_kernel,
        grid=(N_TOK // tm,),
        in_specs=[
            pl.BlockSpec((tm, D_MODEL), row),
            pl.BlockSpec((D_MODEL, D_MODEL), fixed),
            pl.BlockSpec((tm, D_MODEL), row),
            pl.BlockSpec((1, 1, D_MODEL), cond),
            pl.BlockSpec((1, D_MODEL), fixed),
            pl.BlockSpec((1, 1, D_MODEL), cond),
            pl.BlockSpec((1, 1, D_MODEL), cond),
            pl.BlockSpec((D_MODEL, LANE), fixed),
            pl.BlockSpec((D_MODEL, LANE), fixed),
            pl.BlockSpec((1, LANE), fixed),
        ],
        out_specs=[
            pl.BlockSpec((tm, D_MODEL), row),
            pl.BlockSpec((tm, D_MODEL), row),
            pl.BlockSpec((tm, LANE), row),
        ],
        out_shape=[
            jax.ShapeDtypeStruct((N_TOK, D_MODEL), F32),
            jax.ShapeDtypeStruct((N_TOK, D_MODEL), BF16),
            jax.ShapeDtypeStruct((N_TOK, LANE), F32),
        ],
        compiler_params=_cparams(("arbitrary",)),
        name="outproj",
    )(merged, w_out, x, g1.reshape(N_COND, 1, D_MODEL), n2.reshape(1, D_MODEL),
      sc2.reshape(N_COND, 1, D_MODEL), sh2.reshape(N_COND, 1, D_MODEL), wr_hi, wr_lo, br)


def _moe_gu_kernel(be_ref, nb_ref, x_ref, wg_ref, wu_ref, bg_ref, bu_ref, o_ref):
    b = pl.program_id(1)

    @pl.when(b < nb_ref[0])
    def _():
        x = x_ref[...]
        gate = jnp.dot(x, wg_ref[0, 0].astype(BF16), preferred_element_type=F32) + bg_ref[0, 0]
        up = jnp.dot(x, wu_ref[0, 0].astype(BF16), preferred_element_type=F32) + bu_ref[0, 0]
        gate = jnp.minimum(gate, SWIGLU_LIMIT)
        up = jnp.clip(up, -SWIGLU_LIMIT, SWIGLU_LIMIT)
        act = (up + 1.0) * gate * jax.nn.sigmoid(SWIGLU_ALPHA * gate)
        o_ref[...] = act.astype(o_ref.dtype)


def _moe_gu(blk_expert, n_used, x_rows, w_gu, b_gu, layer):
    tn = 512
    nj = D_EXPERT // tn
    grid_spec = pltpu.PrefetchScalarGridSpec(
        num_scalar_prefetch=2,
        grid=(nj, MOE_BLOCKS),
        in_specs=[
            pl.BlockSpec((MOE_BM, D_MODEL), lambda j, b, be, nb: (b, 0)),
            pl.BlockSpec((1, 1, D_MODEL, tn), lambda j, b, be, nb: (layer, be[b], 0, j)),
            pl.BlockSpec((1, 1, D_MODEL, tn), lambda j, b, be, nb: (layer, be[b], 0, nj + j)),
            pl.BlockSpec((1, 1, 1, tn), lambda j, b, be, nb: (layer, be[b], 0, j)),
            pl.BlockSpec((1, 1, 1, tn), lambda j, b, be, nb: (layer, be[b], 0, nj + j)),
        ],
        out_specs=pl.BlockSpec((MOE_BM, tn), lambda j, b, be, nb: (b, j)),
    )
    b4 = b_gu.reshape(DEPTH, N_EXPERTS, 1, 2 * D_EXPERT)
    return pl.pallas_call(
        _moe_gu_kernel,
        grid_spec=grid_spec,
        out_shape=jax.ShapeDtypeStruct((MOE_ROWS, D_EXPERT), BF16),
        compiler_params=_cparams(("arbitrary", "arbitrary")),
        name="moe_gu",
    )(blk_expert, n_used, x_rows, w_gu, w_gu, b4, b4)


def _moe_down_kernel(be_ref, nb_ref, a_ref, w_ref, b_ref, o_ref):
    b = pl.program_id(1)

    @pl.when(b < nb_ref[0])
    def _():
        y = jnp.dot(a_ref[...], w_ref[0, 0].astype(BF16), preferred_element_type=F32) + b_ref[0, 0]
        o_ref[...] = y.astype(o_ref.dtype)


def _moe_down(blk_expert, n_used, act, w_down, b_down, layer):
    tn = 1024
    nj = D_MODEL // tn
    grid_spec = pltpu.PrefetchScalarGridSpec(
        num_scalar_prefetch=2,
        grid=(nj, MOE_BLOCKS),
        in_specs=[
            pl.BlockSpec((MOE_BM, D_EXPERT), lambda j, b, be, nb: (b, 0)),
            pl.BlockSpec((1, 1, D_EXPERT, tn), lambda j, b, be, nb: (layer, be[b], 0, j)),
            pl.BlockSpec((1, 1, 1, tn), lambda j, b, be, nb: (layer, be[b], 0, j)),
        ],
        out_specs=pl.BlockSpec((MOE_BM, tn), lambda j, b, be, nb: (b, j)),
    )
    return pl.pallas_call(
        _moe_down_kernel,
        grid_spec=grid_spec,
        out_shape=jax.ShapeDtypeStruct((MOE_ROWS, D_MODEL), F32),
        compiler_params=_cparams(("arbitrary", "arbitrary")),
        name="moe_down",
    )(blk_expert, n_used, act, w_down, b_down.reshape(DEPTH, N_EXPERTS, 1, D_MODEL))


def _route(logits):
    t = N_TOK
    top_v, top_i = lax.top_k(logits[:, :N_EXPERTS], TOP_K)
    top_w = jax.nn.softmax(top_v, axis=-1)
    onehot = (top_i[:, :, None] == jnp.arange(N_EXPERTS)[None, None, :]).astype(jnp.int32)
    per_tok = jnp.sum(onehot, axis=1)
    before = jnp.cumsum(per_tok, axis=0) - per_tok
    rank = jnp.sum(before[:, None, :] * onehot, axis=-1)
    counts = jnp.sum(per_tok, axis=0)
    padded = (counts + MOE_BM - 1) // MOE_BM * MOE_BM
    pad_end = jnp.cumsum(padded)
    pad_start = pad_end - padded
    dest = pad_start[top_i] + rank
    tok = jnp.broadcast_to(jnp.arange(t, dtype=jnp.int32)[:, None], (t, TOP_K))
    row_tok = jnp.zeros((MOE_ROWS,), jnp.int32).at[dest.reshape(-1)].set(tok.reshape(-1))
    blk_expert = jnp.minimum(
        jnp.searchsorted(pad_end, jnp.arange(MOE_BLOCKS) * MOE_BM, side='right'), N_EXPERTS - 1).astype(jnp.int32)
    n_used = (pad_end[-1:] // MOE_BM).astype(jnp.int32)
    return top_w, dest, row_tok, blk_expert, n_used


def _rms(x, g):
    xf = x.astype(F32)
    return xf * lax.rsqrt(jnp.mean(xf * xf, axis=-1, keepdims=True) + NORM_EPS) * g.astype(F32)


def _rope_tables(n):
    t = jnp.arange(n)
    row = (t // GRID_W).astype(F32)
    col = (t % GRID_W).astype(F32)
    quarter = DA_HEADDIM // 4
    inv_freq = ROPE_BASE ** (-jnp.arange(quarter, dtype=F32) / quarter)
    ang_r = row[:, None] * inv_freq
    ang_c = col[:, None] * inv_freq
    ang = jnp.concatenate([ang_r, ang_r, ang_c, ang_c], axis=-1)
    return jnp.cos(ang), jnp.sin(ang)


def _rope(x, cos, sin):
    a, b, c, d = jnp.split(x, 4, axis=-1)
    rot = jnp.concatenate([-b, a, -d, c], axis=-1)
    return x * cos[None, :, None, None, :] + rot * sin[None, :, None, None, :]


def _conv_silu(x, w, b, nseq, n):
    x = x.reshape(nseq, n, -1).astype(F32)
    pad = (SSD_CONV - 1) // 2
    xp = jnp.pad(x, ((0, 0), (pad, pad), (0, 0)))
    y = sum(xp[:, k:k + n, :] * w[k][None, None, :] for k in range(SSD_CONV)) + b
    return jax.nn.silu(y).reshape(nseq * n, -1)


def _layer(l, x, mod, p, cache_k, cache_v, h0_t):
    sh1, sc1, g1, sh2, sc2, g2 = [mod[l, :N_COND, i * D_MODEL:(i + 1) * D_MODEL] for i in range(6)]
    w_in = p['w_in'][l]
    o_dt = 2 * BRANCH_WIDTH + 2 * SSD_BC
    o_q = o_dt + 2 * SSD_HEADS
    o_u = o_q + 3 * BRANCH_WIDTH
    o_g = o_u + 2 * BRANCH_WIDTH
    wdt = jnp.pad(w_in[:, o_dt:o_q], ((0, 0), (0, LANE - 2 * SSD_HEADS))).astype(BF16)
    h, dt_raw = _prologue(x, p['norm1_g'][l], sc1, sh1, wdt)

    zx = _matmul(h, w_in[:, :o_dt].astype(BF16), 1024, 1152, BF16, "inproj_ssd")
    qkv = _matmul(h, w_in[:, o_q:o_u].astype(BF16), 1024, 1024, BF16, "inproj_attn")
    uv = _matmul(h, w_in[:, o_u:o_g].astype(BF16), 1024, 1024, BF16, "inproj_sgu")
    gates = _matmul(h, w_in[:, o_g:].astype(BF16), 1024, 1024, BF16, "inproj_gates")

    z = zx[:, :BRANCH_WIDTH]
    xbc = zx[:, BRANCH_WIDTH:]
    xc = jnp.concatenate([
        _conv_silu(xbc[:N_PROMPT], p['conv_w'][l], p['conv_b'][l], BATCH, SEQ),
        _conv_silu(xbc[N_PROMPT:], p['conv_w'][l], p['conv_b'][l], DEC_BATCH, DEC_SEQ)], axis=0)
    dt_bias = jnp.pad(p['dt_bias'][l].reshape(1, 2 * SSD_HEADS), ((0, 0), (0, LANE - 2 * SSD_HEADS)))
    a_neg = jnp.pad(-jnp.exp(p['a_log'][l].reshape(1, 2 * SSD_HEADS)), ((0, 0), (0, LANE - 2 * SSD_HEADS)))
    y_f, y_b, h_fin = _ssd(xc, dt_raw, dt_bias, a_neg, h0_t[:, l])
    xs = xc[:, :BRANCH_WIDTH]
    d_full = jnp.repeat(p['d_skip'][l], SSD_HEADDIM)
    y = (y_f + y_b + d_full[None, :] * xs) * jax.nn.silu(z.astype(F32))
    y_a = _rms(y, p['ssm_norm_g'][l]).astype(BF16)
    state_l = jnp.swapaxes(h_fin[:BATCH], -1, -2)

    qg, kg = p['qk_norm_g'][l][0], p['qk_norm_g'][l][1]
    q5 = _rms(qkv[:, :BRANCH_WIDTH].reshape(N_TOK, DA_HEADS, 2, DA_HEADDIM), qg)
    k5 = _rms(qkv[:, BRANCH_WIDTH:2 * BRANCH_WIDTH].reshape(N_TOK, DA_HEADS, 2, DA_HEADDIM), kg)
    v = qkv[:, 2 * BRANCH_WIDTH:]
    k_store = k5[:N_PROMPT].reshape(BATCH, SEQ, DA_HEADS, 2 * DA_HEADDIM)
    v_store = v[:N_PROMPT].astype(F32).reshape(BATCH, SEQ, DA_HEADS, 2 * DA_HEADDIM)
    lam_init = 0.8 - 0.6 * math.exp(-0.3 * l)
    lv = p['lam'][l]
    lam = (jnp.exp(jnp.sum(lv[0] * lv[1])) - jnp.exp(jnp.sum(lv[2] * lv[3])) + lam_init).reshape(1)
    qp = q5[:N_PROMPT].reshape(N_PROMPT, BRANCH_WIDTH).astype(BF16)
    kp = k5[:N_PROMPT].reshape(N_PROMPT, BRANCH_WIDTH).astype(BF16)
    o_p = _attention(lam, qp, kp, v[:N_PROMPT], p['subln_g'][l], nseq=BATCH, lq=SEQ, lk=SEQ,
                     lam_init=lam_init, name="attn_prompt")
    cos, sin = _rope_tables(DEC_SEQ)
    qs = _rope(q5[N_PROMPT:].reshape(DEC_BATCH, DEC_SEQ, DA_HEADS, 2, DA_HEADDIM), cos, sin)
    ks = _rope(k5[N_PROMPT:].reshape(DEC_BATCH, DEC_SEQ, DA_HEADS, 2, DA_HEADDIM), cos, sin)
    lk = DEC_SEQ + PAST_LEN
    k_all = jnp.concatenate([ks.reshape(DEC_BATCH, DEC_SEQ, BRANCH_WIDTH),
                             cache_k[:, l].reshape(DEC_BATCH, PAST_LEN, BRANCH_WIDTH)], axis=1)
    v_all = jnp.concatenate([v[N_PROMPT:].reshape(DEC_BATCH, DEC_SEQ, BRANCH_WIDTH).astype(F32),
                             cache_v[:, l].reshape(DEC_BATCH, PAST_LEN, BRANCH_WIDTH)], axis=1)
    o_s = _attention(lam, qs.reshape(N_SAMPLE, BRANCH_WIDTH).astype(BF16),
                     k_all.reshape(DEC_BATCH * lk, BRANCH_WIDTH).astype(BF16),
                     v_all.reshape(DEC_BATCH * lk, BRANCH_WIDTH).astype(BF16),
                     p['subln_g'][l], nseq=DEC_BATCH, lq=DEC_SEQ, lk=lk, lam_init=lam_init, name="attn_sample")
    y_b2 = jnp.concatenate([o_p, o_s], axis=0)

    y_c = _sgu(uv, uv, p['sgu_norm_g'][l], p['w_s'][l], p['b_s'][l].T)

    merged = _merge(y_a, y_b2, y_c, p['w_branch'][l].astype(BF16), gates)
    wr = jnp.pad(p['w_router'][l], ((0, 0), (0, LANE - N_EXPERTS)))
    wr_hi = wr.astype(BF16)
    wr_lo = (wr - wr_hi.astype(F32)).astype(BF16)
    br = jnp.pad(p['b_router'][l].reshape(1, N_EXPERTS), ((0, 0), (0, LANE - N_EXPERTS)))
    x1, h2, logits = _outproj(merged, p['w_out'][l].astype(BF16), x, g1, p['norm2_g'][l], sc2, sh2,
                              wr_hi, wr_lo, br)

    top_w, dest, row_tok, blk_expert, n_used = _route(logits)
    x_rows = h2[row_tok]
    act = _moe_gu(blk_expert, n_used, x_rows, p['w_gu'], p['b_gu'], l)
    y_rows = _moe_down(blk_expert, n_used, act, p['w_down'], p['b_down'], l)
    moe_out = jnp.sum(y_rows[dest] * top_w[:, :, None], axis=1)
    tile_cond = jnp.concatenate([jnp.zeros((N_PROMPT,), jnp.int32),
                                 1 + jnp.arange(N_SAMPLE, dtype=jnp.int32) // DEC_SEQ])
    x2 = x1 + g2[tile_cond] * moe_out
    return x2, k_store, v_store, state_l


def kernel(x_prompt, x_sample, c, cache_k, cache_v, state_ssm, c_ctx, w_ada, b_ada, norm1_g, norm2_g,
           w_in, conv_w, conv_b, dt_bias, a_log, d_skip, ssm_norm_g, qk_norm_g, lam, subln_g,
           sgu_norm_g, w_s, b_s, w_branch, w_out, w_router, b_router, w_gu, b_gu, w_down, b_down):
    p = {'norm1_g': norm1_g, 'norm2_g': norm2_g, 'w_in': w_in, 'conv_w': conv_w, 'conv_b': conv_b,
         'dt_bias': dt_bias, 'a_log': a_log, 'd_skip': d_skip, 'ssm_norm_g': ssm_norm_g,
         'qk_norm_g': qk_norm_g, 'lam': lam, 'subln_g': subln_g, 'sgu_norm_g': sgu_norm_g, 'w_s': w_s,
         'b_s': b_s, 'w_branch': w_branch, 'w_out': w_out, 'w_router': w_router, 'b_router': b_router,
         'w_gu': w_gu, 'b_gu': b_gu, 'w_down': w_down, 'b_down': b_down}
    cond8 = jnp.concatenate([c_ctx[None, :], c, jnp.zeros((SUBLANE - N_COND, D_MODEL), F32)], axis=0)
    mod = _ada_all(cond8, w_ada, b_ada)
    h0_t = jnp.concatenate([jnp.zeros((1,) + state_ssm.shape[1:], F32), state_ssm], axis=0)
    h0_t = jnp.swapaxes(h0_t, -1, -2)
    x = jnp.concatenate([x_prompt.reshape(N_PROMPT, D_MODEL), x_sample.reshape(N_SAMPLE, D_MODEL)], axis=0)
    ks, vs, ss = [], [], []
    for l in range(DEPTH):
        x, k_l, v_l, s_l = _layer(l, x, mod, p, cache_k, cache_v, h0_t)
        ks.append(k_l)
        vs.append(v_l)
        ss.append(s_l)
    y_prompt = x[:N_PROMPT].reshape(BATCH, SEQ, D_MODEL)
    y_sample = x[N_PROMPT:].reshape(DEC_BATCH, DEC_SEQ, D_MODEL)
    return (y_prompt, y_sample, jnp.stack(ks, axis=1), jnp.stack(vs, axis=1), jnp.stack(ss, axis=1))
```

```python
import functools
import math

import jax
import jax.numpy as jnp
import numpy as np
from jax import lax
from jax.experimental import pallas as pl
from jax.experimental.pallas import tpu as pltpu

F32 = jnp.float32
BF16 = jnp.bfloat16

D_MODEL = 2048
BATCH = 32
SEQ = 256
DEPTH = 2
DEC_BATCH = 4
DEC_SEQ = 2048
PAST_LEN = 512
GRID_W = 64
BRANCH_WIDTH = 1024
SSD_HEADDIM = 64
SSD_HEADS = 16
SSD_GROUPS = 2
SSD_STATE = 64
SSD_CONV = 7
SSD_CHUNK = 128
SSD_BC = 128
DA_HEADS = 8
DA_HEADDIM = 64
ROPE_BASE = 10000.0
SG_CHUNK = 128
SG_GROUPS = 8
N_EXPERTS = 32
TOP_K = 4
D_EXPERT = 2048
SWIGLU_LIMIT = 7.0
SWIGLU_ALPHA = 1.702
NORM_EPS = 1e-6

N_PROMPT = BATCH * SEQ
N_SAMPLE = DEC_BATCH * DEC_SEQ
N_TOK = N_PROMPT + N_SAMPLE
N_COND = 1 + DEC_BATCH

LANE = 128
SUBLANE = 8
VMEM_LIMIT = 56 * 1024 * 1024

MOE_BM = 512
MOE_BLOCKS = N_TOK * TOP_K // MOE_BM + N_EXPERTS
MOE_ROWS = MOE_BLOCKS * MOE_BM


def _cparams(sem):
    return pltpu.CompilerParams(dimension_semantics=sem, vmem_limit_bytes=VMEM_LIMIT)


def _cond_of_tile(i, tm):
    row = i * tm
    return jnp.where(row < N_PROMPT, 0, 1 + (row - N_PROMPT) // DEC_SEQ)


def _ada_kernel(c_ref, w_ref, b_ref, o_ref):
    c = c_ref[...]
    s = (c * jax.nn.sigmoid(c)).astype(BF16)
    o_ref[0] = jnp.dot(s, w_ref[0].astype(BF16), preferred_element_type=F32) + b_ref[0]


def _ada_all(cond8, w_ada, b_ada):
    tn = 1024
    n = 6 * D_MODEL
    return pl.pallas_call(
        _ada_kernel,
        grid=(DEPTH, n // tn),
        in_specs=[
            pl.BlockSpec((SUBLANE, D_MODEL), lambda l, j: (0, 0)),
            pl.BlockSpec((1, D_MODEL, tn), lambda l, j: (l, 0, j)),
            pl.BlockSpec((1, 1, tn), lambda l, j: (l, 0, j)),
        ],
        out_specs=pl.BlockSpec((1, SUBLANE, tn), lambda l, j: (l, 0, j)),
        out_shape=jax.ShapeDtypeStruct((DEPTH, SUBLANE, n), F32),
        compiler_params=_cparams(("arbitrary", "arbitrary")),
        name="ada",
    )(cond8, w_ada, b_ada.reshape(DEPTH, 1, n))


def _prologue_kernel(x_ref, g_ref, sc_ref, sh_ref, wdt_ref, h_ref, dt_ref):
    x = x_ref[...]
    ms = jnp.mean(x * x, axis=-1, keepdims=True)
    y = x * lax.rsqrt(ms + NORM_EPS) * g_ref[...]
    h = (y * (1.0 + sc_ref[0]) + sh_ref[0]).astype(BF16)
    h_ref[...] = h
    dt_ref[...] = jnp.dot(h, wdt_ref[...], preferred_element_type=F32)


def _prologue(x, g, sc, sh, wdt):
    tm = 512
    return pl.pallas_call(
        _prologue_kernel,
        grid=(N_TOK // tm,),
        in_specs=[
            pl.BlockSpec((tm, D_MODEL), lambda i: (i, 0)),
            pl.BlockSpec((1, D_MODEL), lambda i: (0, 0)),
            pl.BlockSpec((1, 1, D_MODEL), lambda i: (_cond_of_tile(i, tm), 0, 0)),
            pl.BlockSpec((1, 1, D_MODEL), lambda i: (_cond_of_tile(i, tm), 0, 0)),
            pl.BlockSpec((D_MODEL, LANE), lambda i: (0, 0)),
        ],
        out_specs=[
            pl.BlockSpec((tm, D_MODEL), lambda i: (i, 0)),
            pl.BlockSpec((tm, LANE), lambda i: (i, 0)),
        ],
        out_shape=[
            jax.ShapeDtypeStruct((N_TOK, D_MODEL), BF16),
            jax.ShapeDtypeStruct((N_TOK, LANE), F32),
        ],
        compiler_params=_cparams(("arbitrary",)),
        name="prologue",
    )(x, g.reshape(1, D_MODEL), sc.reshape(N_COND, 1, D_MODEL), sh.reshape(N_COND, 1, D_MODEL), wdt)


def _mm_kernel(a_ref, w_ref, o_ref):
    o_ref[...] = jnp.dot(a_ref[...], w_ref[...], preferred_element_type=F32).astype(o_ref.dtype)


def _matmul(a, w, tm, tn, out_dtype, name):
    m, k = a.shape
    n = w.shape[1]
    return pl.pallas_call(
        _mm_kernel,
        grid=(m // tm, n // tn),
        in_specs=[
            pl.BlockSpec((tm, k), lambda i, j: (i, 0)),
            pl.BlockSpec((k, tn), lambda i, j: (0, j)),
        ],
        out_specs=pl.BlockSpec((tm, tn), lambda i, j: (i, j)),
        out_shape=jax.ShapeDtypeStruct((m, n), out_dtype),
        compiler_params=_cparams(("arbitrary", "arbitrary")),
        name=name,
    )(a, w)


CONV_TM = 256
CONV_HALO = 16


def _conv_kernel(x_ref, prev_ref, next_ref, w_ref, b_ref, o_ref, pad_ref):
    i = pl.program_id(0)
    n_prompt_tiles = N_PROMPT // CONV_TM
    tiles_per_seq = DEC_SEQ // CONV_TM
    j = lax.rem(jnp.maximum(i - n_prompt_tiles, 0), tiles_per_seq)
    is_prompt = i < n_prompt_tiles
    first = jnp.logical_or(is_prompt, j == 0)
    last = jnp.logical_or(is_prompt, j == tiles_per_seq - 1)
    pad_ref[0:CONV_HALO, :] = jnp.where(first, 0.0, prev_ref[...].astype(F32))
    pad_ref[CONV_HALO:CONV_HALO + CONV_TM, :] = x_ref[...].astype(F32)
    pad_ref[CONV_HALO + CONV_TM:, :] = jnp.where(last, 0.0, next_ref[...].astype(F32))
    acc = jnp.broadcast_to(b_ref[...], (CONV_TM, b_ref.shape[1]))
    half = (SSD_CONV - 1) // 2
    for k in range(SSD_CONV):
        acc = acc + pad_ref[pl.ds(CONV_HALO - half + k, CONV_TM), :] * w_ref[k:k + 1, :]
    o_ref[...] = (acc * jax.nn.sigmoid(acc)).astype(o_ref.dtype)


def _conv_silu(xbc, w, b):
    wx = xbc.shape[1]
    hb = CONV_TM // CONV_HALO
    last_halo = N_TOK // CONV_HALO - 1
    return pl.pallas_call(
        _conv_kernel,
        grid=(N_TOK // CONV_TM,),
        in_specs=[
            pl.BlockSpec((CONV_TM, wx), lambda i: (i, 0)),
            pl.BlockSpec((CONV_HALO, wx), lambda i: (jnp.maximum(i * hb - 1, 0), 0)),
            pl.BlockSpec((CONV_HALO, wx), lambda i: (jnp.minimum((i + 1) * hb, last_halo), 0)),
            pl.BlockSpec((SSD_CONV, wx), lambda i: (0, 0)),
            pl.BlockSpec((1, wx), lambda i: (0, 0)),
        ],
        out_specs=pl.BlockSpec((CONV_TM, wx), lambda i: (i, 0)),
        out_shape=jax.ShapeDtypeStruct((N_TOK, wx), BF16),
        scratch_shapes=[pltpu.VMEM((CONV_TM + 2 * CONV_HALO, wx), F32)],
        compiler_params=_cparams(("arbitrary",)),
        name="ssd_conv",
    )(xbc, xbc, xbc, w, b.reshape(1, wx))


def _ssd_post_kernel(yf_ref, yb_ref, xs_ref, z_ref, d_ref, g_ref, o_ref):
    z = z_ref[...].astype(F32)
    y = yf_ref[...] + yb_ref[...] + d_ref[...] * xs_ref[...].astype(F32)
    y = y * (z * jax.nn.sigmoid(z))
    y = y * lax.rsqrt(jnp.mean(y * y, axis=-1, keepdims=True) + NORM_EPS) * g_ref[...]
    o_ref[...] = y.astype(o_ref.dtype)


def _ssd_post(y_f, y_b, xc, z, d_full, g):
    tm = 512
    row = lambda i: (i, 0)
    fixed = lambda i: (0, 0)
    return pl.pallas_call(
        _ssd_post_kernel,
        grid=(N_TOK // tm,),
        in_specs=[
            pl.BlockSpec((tm, BRANCH_WIDTH), row),
            pl.BlockSpec((tm, BRANCH_WIDTH), row),
            pl.BlockSpec((tm, BRANCH_WIDTH), row),
            pl.BlockSpec((tm, BRANCH_WIDTH), row),
            pl.BlockSpec((1, BRANCH_WIDTH), fixed),
            pl.BlockSpec((1, BRANCH_WIDTH), fixed),
        ],
        out_specs=pl.BlockSpec((tm, BRANCH_WIDTH), row),
        out_shape=jax.ShapeDtypeStruct((N_TOK, BRANCH_WIDTH), BF16),
        compiler_params=_cparams(("arbitrary",)),
        name="ssd_post",
    )(y_f, y_b, xc, z, d_full.reshape(1, BRANCH_WIDTH), g.reshape(1, BRANCH_WIDTH))


def _split3(x):
    hi = x.astype(BF16)
    r = x - hi.astype(F32)
    mid = r.astype(BF16)
    lo = (r - mid.astype(F32)).astype(BF16)
    return hi, mid, lo


def _ssd_kernel(fwd_ref, bwd_ref, first_ref, seq_ref, h0i_ref,
                xf_ref, xb_ref, dtf_ref, dtb_ref, bias_ref, a_ref, h0_ref,
                yf_ref, yb_ref, hout_ref, hstate):
    s = pl.program_id(0)
    q = SSD_CHUNK

    @pl.when(first_ref[s] == 1)
    def _():
        hstate[...] = h0_ref[0]

    ri = lax.broadcasted_iota(jnp.int32, (q, q), 0)
    ci = lax.broadcasted_iota(jnp.int32, (q, q), 1)

    for d in (0, 1):
        x_ref, dt_ref, y_ref = ((xf_ref, dtf_ref, yf_ref), (xb_ref, dtb_ref, yb_ref))[d]
        xbc = x_ref[...].astype(F32)
        dt = jax.nn.softplus(dt_ref[...] + bias_ref[...])
        adt = dt * a_ref[...]
        mask = (ri >= ci) if d == 0 else (ri <= ci)
        tri = jnp.where(mask, 1.0, 0.0).astype(BF16)
        hi, mid, lo = _split3(adt)
        cum = (jnp.dot(tri, hi, preferred_element_type=F32)
               + jnp.dot(tri, mid, preferred_element_type=F32)
               + jnp.dot(tri, lo, preferred_element_type=F32))
        cum_t = cum.T
        dt_t = dt.T
        tot = cum[q - 1:q, :] if d == 0 else cum[0:1, :]
        bm = xbc[:, 1024:1152]
        cm = xbc[:, 1152:1280]
        bm_t = bm.T
        ys = []
        for g in range(SSD_GROUPS):
            cg = cm[:, g * SSD_STATE:(g + 1) * SSD_STATE]
            bg = bm[:, g * SSD_STATE:(g + 1) * SSD_STATE]
            cb = lax.dot_general(cg, bg, (((1,), (1,)), ((), ())), preferred_element_type=F32)
            bg_t = bm_t[g * SSD_STATE:(g + 1) * SSD_STATE, :]
            for hh in range(SSD_HEADS // SSD_GROUPS):
                h = g * (SSD_HEADS // SSD_GROUPS) + hh
                ch = d * SSD_HEADS + h
                pcol = cum[:, ch:ch + 1]
                prow = cum_t[ch:ch + 1, :]
                dtrow = dt_t[ch:ch + 1, :]
                lm = jnp.exp(jnp.where(mask, pcol - prow, -1e30))
                scores = cb * lm * dtrow
                xs_h = xbc[:, h * SSD_HEADDIM:(h + 1) * SSD_HEADDIM]
                h_t = hstate[d, h]
                y = jnp.dot(scores, xs_h, preferred_element_type=F32)
                y = y + jnp.dot(cg, h_t, preferred_element_type=F32) * jnp.exp(pcol)
                tot_h = tot[:, ch:ch + 1]
                w_row = dtrow * jnp.exp(tot_h - prow)
                hstate[d, h] = jnp.exp(tot_h) * h_t + jnp.dot(bg_t * w_row, xs_h, preferred_element_type=F32)
                ys.append(y)
        y_ref[...] = jnp.concatenate(ys, axis=1)
    hout_ref[0] = hstate[...]


def _ssd_tables():
    fwd, bwd, first, seq, h0i = [], [], [], [], []
    cpr = SEQ // SSD_CHUNK
    for b in range(BATCH):
        for c in range(cpr):
            fwd.append(b * cpr + c)
            bwd.append(b * cpr + (cpr - 1 - c))
            first.append(1 if c == 0 else 0)
            seq.append(b)
            h0i.append(0)
    base = N_PROMPT // SSD_CHUNK
    cps = DEC_SEQ // SSD_CHUNK
    for b in range(DEC_BATCH):
        for c in range(cps):
            fwd.append(base + b * cps + c)
            bwd.append(base + b * cps + (cps - 1 - c))
            first.append(1 if c == 0 else 0)
            seq.append(BATCH + b)
            h0i.append(1 + b)
    return [jnp.asarray(np.asarray(t, np.int32)) for t in (fwd, bwd, first, seq, h0i)]


def _ssd(xc, dt_raw, dt_bias128, a128, h0_t):
    q = SSD_CHUNK
    nsteps = N_TOK // q
    wx = BRANCH_WIDTH + 2 * SSD_BC
    st_shape = (2, SSD_HEADS, SSD_STATE, SSD_HEADDIM)
    grid_spec = pltpu.PrefetchScalarGridSpec(
        num_scalar_prefetch=5,
        grid=(nsteps,),
        in_specs=[
            pl.BlockSpec((q, wx), lambda s, f, b, fi, sq, hi: (f[s], 0)),
            pl.BlockSpec((q, wx), lambda s, f, b, fi, sq, hi: (b[s], 0)),
            pl.BlockSpec((q, LANE), lambda s, f, b, fi, sq, hi: (f[s], 0)),
            pl.BlockSpec((q, LANE), lambda s, f, b, fi, sq, hi: (b[s], 0)),
            pl.BlockSpec((1, LANE), lambda s, f, b, fi, sq, hi: (0, 0)),
            pl.BlockSpec((1, LANE), lambda s, f, b, fi, sq, hi: (0, 0)),
            pl.BlockSpec((1,) + st_shape, lambda s, f, b, fi, sq, hi: (hi[s], 0, 0, 0, 0)),
        ],
        out_specs=[
            pl.BlockSpec((q, BRANCH_WIDTH), lambda s, f, b, fi, sq, hi: (f[s], 0)),
            pl.BlockSpec((q, BRANCH_WIDTH), lambda s, f, b, fi, sq, hi: (b[s], 0)),
            pl.BlockSpec((1,) + st_shape, lambda s, f, b, fi, sq, hi: (sq[s], 0, 0, 0, 0)),
        ],
        scratch_shapes=[pltpu.VMEM(st_shape, F32)],
    )
    return pl.pallas_call(
        _ssd_kernel,
        grid_spec=grid_spec,
        out_shape=[
            jax.ShapeDtypeStruct((N_TOK, BRANCH_WIDTH), F32),
            jax.ShapeDtypeStruct((N_TOK, BRANCH_WIDTH), F32),
            jax.ShapeDtypeStruct((BATCH + DEC_BATCH,) + st_shape, F32),
        ],
        compiler_params=_cparams(("arbitrary",)),
        name="ssd",
    )(*_ssd_tables(), xc, xc, dt_raw, dt_raw, dt_bias128, a128, h0_t)


HEAD_W = 2 * DA_HEADDIM
QKV_V_BLOCK = 2 * DA_HEADS
ATTN_PRE_TM = 512


def _attn_pre_kernel(*refs, rope, store):
    q_ref, k_ref, v_ref, gq_ref, gk_ref, gm_ref = refs[:6]
    refs = refs[6:]
    if rope:
        cos_ref, sa_ref, sb_ref = refs[:3]
        refs = refs[3:]
    qo_ref, ko_ref = refs[:2]
    if store:
        ks_ref, vs_ref = refs[2:4]
        vs_ref[...] = v_ref[...].astype(F32)
    gm = gm_ref[...]
    for x_ref, g_ref, o_ref, is_k in ((q_ref, gq_ref, qo_ref, False), (k_ref, gk_ref, ko_ref, True)):
        for s in range(DA_HEADS):
            cols = slice(s * HEAD_W, (s + 1) * HEAD_W)
            xs = x_ref[:, cols].astype(F32)
            xx = xs * xs
            hi = xx.astype(BF16)
            lo = (xx - hi.astype(F32)).astype(BF16)
            ms = jnp.dot(hi, gm, preferred_element_type=F32) + jnp.dot(lo, gm, preferred_element_type=F32)
            xn = xs * lax.rsqrt(ms + NORM_EPS) * g_ref[...]
            if store and is_k:
                ks_ref[:, cols] = xn
            if rope:
                xn = (xn * cos_ref[...] + pltpu.roll(xn, HEAD_W - 16, 1) * sa_ref[...]
                      + pltpu.roll(xn, 16, 1) * sb_ref[...])
            o_ref[:, cols] = xn.astype(o_ref.dtype)


def _attn_pre(qkv, gq, gk, gm, tables, *, row_off, nrows, store, name):
    tm = ATTN_PRE_TM
    off = row_off // tm
    rope = tables is not None
    in_specs = [
        pl.BlockSpec((tm, BRANCH_WIDTH), lambda i: (i + off, 0)),
        pl.BlockSpec((tm, BRANCH_WIDTH), lambda i: (i + off, 1)),
        pl.BlockSpec((tm, BRANCH_WIDTH), lambda i: (i + off, 2)),
        pl.BlockSpec((1, HEAD_W), lambda i: (0, 0)),
        pl.BlockSpec((1, HEAD_W), lambda i: (0, 0)),
        pl.BlockSpec((HEAD_W, HEAD_W), lambda i: (0, 0)),
    ]
    args = [qkv, qkv, qkv, gq, gk, gm]
    if rope:
        tps = DEC_SEQ // tm
        in_specs += [pl.BlockSpec((tm, HEAD_W), lambda i: (lax.rem(i, tps), 0))] * 3
        args += list(tables)
    out_specs = [pl.BlockSpec((tm, BRANCH_WIDTH), lambda i: (i, 0))] * 2
    out_shape = [jax.ShapeDtypeStruct((nrows, BRANCH_WIDTH), BF16)] * 2
    if store:
        out_specs += [pl.BlockSpec((tm, BRANCH_WIDTH), lambda i: (i, 0))] * 2
        out_shape += [jax.ShapeDtypeStruct((nrows, BRANCH_WIDTH), F32)] * 2
    return pl.pallas_call(
        functools.partial(_attn_pre_kernel, rope=rope, store=store),
        grid=(nrows // tm,),
        in_specs=in_specs,
        out_specs=out_specs,
        out_shape=out_shape,
        compiler_params=_cparams(("arbitrary",)),
        name=name,
    )(*args)


def _attn_kernel(lam_ref, q_ref, *refs, lam_init, nkv):
    k_refs = refs[:nkv]
    v_refs = refs[nkv:2 * nkv]
    g_ref, o_ref = refs[2 * nkv:]
    lam = lam_ref[0]
    qb = q_ref[...]
    ks = [k_ref[...].astype(BF16) for k_ref in k_refs]
    es, coef = [], []
    for m in (0, 1):
        cols = slice(m * DA_HEADDIM, (m + 1) * DA_HEADDIM)
        ss = [lax.dot_general(qb[:, cols], kb[:, cols], (((1,), (1,)), ((), ())),
                              preferred_element_type=F32) for kb in ks]
        mx = functools.reduce(jnp.maximum, [jnp.max(s, axis=-1, keepdims=True) for s in ss])
        em = [jnp.exp2(s - mx) for s in ss]
        tot = functools.reduce(lambda a, b: a + b, [jnp.sum(e, axis=-1, keepdims=True) for e in em])
        es.append(em)
        coef.append(1.0 / tot)
    o = None
    for i in range(nkv):
        w = es[0][i] * coef[0] - es[1][i] * (lam * coef[1])
        t = jnp.dot(w.astype(BF16), v_refs[i][...].astype(BF16), preferred_element_type=F32)
        o = t if o is None else o + t
    o = o * lax.rsqrt(jnp.mean(o * o, axis=-1, keepdims=True) + NORM_EPS) * g_ref[...]
    o_ref[...] = (o * (1.0 - lam_init)).astype(o_ref.dtype)


def _attention(lam, q, ks, vs, kv_specs, subln_g, *, nseq, lq, lam_init, name):
    tq = 256
    nqb = lq // tq
    nkv = len(ks)
    grid_spec = pltpu.PrefetchScalarGridSpec(
        num_scalar_prefetch=0,
        grid=(nseq, DA_HEADS, nqb),
        in_specs=[
            pl.BlockSpec(memory_space=pltpu.SMEM),
            pl.BlockSpec((tq, HEAD_W), lambda b, h, i: (b * nqb + i, h)),
            *kv_specs,
            pl.BlockSpec((1, HEAD_W), lambda b, h, i: (0, 0)),
        ],
        out_specs=pl.BlockSpec((tq, HEAD_W), lambda b, h, i: (b * nqb + i, h)),
    )
    return pl.pallas_call(
        functools.partial(_attn_kernel, lam_init=lam_init, nkv=nkv),
        grid_spec=grid_spec,
        out_shape=jax.ShapeDtypeStruct((nseq * lq, DA_HEADS * HEAD_W), BF16),
        compiler_params=_cparams(("arbitrary", "arbitrary", "arbitrary")),
        name=name,
    )(lam, q, *ks, *vs, subln_g.reshape(1, HEAD_W))


def _sgu_kernel(u_ref, v_ref, g_ref, ws_ref, bt_ref, o_ref, *, chunks):
    v = v_ref[...].astype(F32)
    mu = jnp.mean(v, axis=-1, keepdims=True)
    vc = v - mu
    var = jnp.mean(vc * vc, axis=-1, keepdims=True)
    vn = (vc * lax.rsqrt(var + NORM_EPS) * g_ref[...]).astype(BF16)
    u = u_ref[...].astype(F32)
    gw = BRANCH_WIDTH // SG_GROUPS
    for c in range(chunks):
        rows = slice(c * SG_CHUNK, (c + 1) * SG_CHUNK)
        outs = []
        for g in range(SG_GROUPS):
            cols = slice(g * gw, (g + 1) * gw)
            mixed = jnp.dot(ws_ref[g].astype(BF16), vn[rows, cols], preferred_element_type=F32)
            mixed = mixed + bt_ref[:, g:g + 1]
            outs.append(u[rows, cols] * mixed)
        o_ref[rows, :] = jnp.concatenate(outs, axis=1).astype(o_ref.dtype)


def _sgu(u, v, g, ws, bs_t):
    chunks = 4
    tm = chunks * SG_CHUNK
    return pl.pallas_call(
        functools.partial(_sgu_kernel, chunks=chunks),
        grid=(N_TOK // tm,),
        in_specs=[
            pl.BlockSpec((tm, BRANCH_WIDTH), lambda i: (i, 0)),
            pl.BlockSpec((tm, BRANCH_WIDTH), lambda i: (i, 1)),
            pl.BlockSpec((1, BRANCH_WIDTH), lambda i: (0, 0)),
            pl.BlockSpec((SG_GROUPS, SG_CHUNK, SG_CHUNK), lambda i: (0, 0, 0)),
            pl.BlockSpec((SG_CHUNK, SG_GROUPS), lambda i: (0, 0)),
        ],
        out_specs=pl.BlockSpec((tm, BRANCH_WIDTH), lambda i: (i, 0)),
        out_shape=jax.ShapeDtypeStruct((N_TOK, BRANCH_WIDTH), BF16),
        compiler_params=_cparams(("arbitrary",)),
        name="sgu",
    )(u, v, g.reshape(1, BRANCH_WIDTH), ws, bs_t)


def _merge_kernel(ya_ref, yb_ref, yc_ref, wb_ref, ga_ref, gb_ref, gc_ref, o_ref):
    acc = None
    for k, (y_ref, g_ref) in enumerate(((ya_ref, ga_ref), (yb_ref, gb_ref), (yc_ref, gc_ref))):
        t = jnp.dot(y_ref[...], wb_ref[k], preferred_element_type=F32)
        t = t * jax.nn.sigmoid(g_ref[...].astype(F32))
        acc = t if acc is None else acc + t
    o_ref[...] = acc.astype(o_ref.dtype)


def _merge(ya, yb, yc, wb, gates):
    tm, tn = 1024, 1024
    nj = D_MODEL // tn
    yspec = pl.BlockSpec((tm, BRANCH_WIDTH), lambda i, j: (i, 0))
    return pl.pallas_call(
        _merge_kernel,
        grid=(N_TOK // tm, nj),
        in_specs=[
            yspec, yspec, yspec,
            pl.BlockSpec((3, BRANCH_WIDTH, tn), lambda i, j: (0, 0, j)),
            pl.BlockSpec((tm, tn), lambda i, j: (i, j)),
            pl.BlockSpec((tm, tn), lambda i, j: (i, nj + j)),
            pl.BlockSpec((tm, tn), lambda i, j: (i, 2 * nj + j)),
        ],
        out_specs=pl.BlockSpec((tm, tn), lambda i, j: (i, j)),
        out_shape=jax.ShapeDtypeStruct((N_TOK, D_MODEL), BF16),
        compiler_params=_cparams(("arbitrary", "arbitrary")),
        name="merge",
    )(ya, yb, yc, wb, gates, gates, gates)


def _outproj_kernel(m_ref, w_ref, x_ref, g1_ref, n2_ref, sc_ref, sh_ref, wrh_ref, wrl_ref, br_ref,
                    x1_ref, h2_ref, lg_ref):
    out = jnp.dot(m_ref[...], w_ref[...], preferred_element_type=F32)
    x1 = x_ref[...] + g1_ref[0] * out
    x1_ref[...] = x1
    ms = jnp.mean(x1 * x1, axis=-1, keepdims=True)
    y = x1 * lax.rsqrt(ms + NORM_EPS) * n2_ref[...]
    h2 = y * (1.0 + sc_ref[0]) + sh_ref[0]
    hi = h2.astype(BF16)
    lo = (h2 - hi.astype(F32)).astype(BF16)
    h2_ref[...] = h2
    wh = wrh_ref[...]
    lg = (jnp.dot(hi, wh, preferred_element_type=F32) + jnp.dot(lo, wh, preferred_element_type=F32)
          + jnp.dot(hi, wrl_ref[...], preferred_element_type=F32))
    lg_ref[...] = lg + br_ref[...]


def _outproj(merged, w_out, x, g1, n2, sc2, sh2, wr_hi, wr_lo, br):
    tm = 512
    cond = lambda i: (_cond_of_tile(i, tm), 0, 0)
    row = lambda i: (i, 0)
    fixed = lambda i: (0, 0)
    return pl.pallas_call(
        _outproj_kernel,
        grid=(N_TOK // tm,),
        in_specs=[
            pl.BlockSpec((tm, D_MODEL), row),
            pl.BlockSpec((D_MODEL, D_MODEL), fixed),
            pl.BlockSpec((tm, D_MODEL), row),
            pl.BlockSpec((1, 1, D_MODEL), cond),
            pl.BlockSpec((1, D_MODEL), fixed),
            pl.BlockSpec((1, 1, D_MODEL), cond),
            pl.BlockSpec((1, 1, D_MODEL), cond),
            pl.BlockSpec((D_MODEL, LANE), fixed),
            pl.BlockSpec((D_MODEL, LANE), fixed),
            pl.BlockSpec((1, LANE), fixed),
        ],
        out_specs=[
            pl.BlockSpec((tm, D_MODEL), row),
            pl.BlockSpec((tm, D_MODEL), row),
            pl.BlockSpec((tm, LANE), row),
        ],
        out_shape=[
            jax.ShapeDtypeStruct((N_TOK, D_MODEL), F32),
            jax.ShapeDtypeStruct((N_TOK, D_MODEL), F32),
            jax.ShapeDtypeStruct((N_TOK, LANE), F32),
        ],
        compiler_params=_cparams(("arbitrary",)),
        name="outproj",
    )(merged, w_out, x, g1.reshape(N_COND, 1, D_MODEL), n2.reshape(1, D_MODEL),
      sc2.reshape(N_COND, 1, D_MODEL), sh2.reshape(N_COND, 1, D_MODEL), wr_hi, wr_lo, br)


def _used(b, nb):
    return jnp.minimum(b, nb[0] - 1)


def _moe_gu_kernel(be_ref, nb_ref, fr_ref, x_ref, wg_ref, wu_ref, bg_ref, bu_ref, o_ref, wg_bf, wu_bf):
    b = pl.program_id(1)

    @pl.when(b >= nb_ref[0])
    def _():
        o_ref[...] = jnp.zeros_like(o_ref)

    @pl.when(jnp.logical_and(b < nb_ref[0], fr_ref[b] == 1))
    def _():
        wg_bf[...] = wg_ref[0, 0].astype(BF16)
        wu_bf[...] = wu_ref[0, 0].astype(BF16)

    @pl.when(b < nb_ref[0])
    def _():
        x = x_ref[...].astype(BF16)
        gate = jnp.dot(x, wg_bf[...], preferred_element_type=F32) + bg_ref[0, 0]
        up = jnp.dot(x, wu_bf[...], preferred_element_type=F32) + bu_ref[0, 0]
        gate = jnp.minimum(gate, SWIGLU_LIMIT)
        up = jnp.clip(up, -SWIGLU_LIMIT, SWIGLU_LIMIT)
        act = (up + 1.0) * gate * jax.nn.sigmoid(SWIGLU_ALPHA * gate)
        o_ref[...] = act.astype(o_ref.dtype)


def _moe_gu(blk_expert, n_used, blk_first, x_rows, w_gu, b_gu, layer):
    tn = 512
    nj = D_EXPERT // tn
    grid_spec = pltpu.PrefetchScalarGridSpec(
        num_scalar_prefetch=3,
        grid=(nj, MOE_BLOCKS),
        in_specs=[
            pl.BlockSpec((MOE_BM, D_MODEL), lambda j, b, be, nb, fr: (_used(b, nb), 0)),
            pl.BlockSpec((1, 1, D_MODEL, tn), lambda j, b, be, nb, fr: (layer, be[_used(b, nb)], 0, j)),
            pl.BlockSpec((1, 1, D_MODEL, tn), lambda j, b, be, nb, fr: (layer, be[_used(b, nb)], 0, nj + j)),
            pl.BlockSpec((1, 1, 1, tn), lambda j, b, be, nb, fr: (layer, be[_used(b, nb)], 0, j)),
            pl.BlockSpec((1, 1, 1, tn), lambda j, b, be, nb, fr: (layer, be[_used(b, nb)], 0, nj + j)),
        ],
        out_specs=pl.BlockSpec((MOE_BM, tn), lambda j, b, be, nb, fr: (b, j)),
        scratch_shapes=[pltpu.VMEM((D_MODEL, tn), BF16), pltpu.VMEM((D_MODEL, tn), BF16)],
    )
    b4 = b_gu.reshape(DEPTH, N_EXPERTS, 1, 2 * D_EXPERT)
    return pl.pallas_call(
        _moe_gu_kernel,
        grid_spec=grid_spec,
        out_shape=jax.ShapeDtypeStruct((MOE_ROWS, D_EXPERT), BF16),
        compiler_params=_cparams(("arbitrary", "arbitrary")),
        name="moe_gu",
    )(blk_expert, n_used, blk_first, x_rows, w_gu, w_gu, b4, b4)


def _moe_down_kernel(be_ref, nb_ref, fr_ref, a_ref, w_ref, b_ref, o_ref, w_bf):
    b = pl.program_id(1)

    @pl.when(b >= nb_ref[0])
    def _():
        o_ref[...] = jnp.zeros_like(o_ref)

    @pl.when(jnp.logical_and(b < nb_ref[0], fr_ref[b] == 1))
    def _():
        w_bf[...] = w_ref[0, 0].astype(BF16)

    @pl.when(b < nb_ref[0])
    def _():
        y = jnp.dot(a_ref[...], w_bf[...], preferred_element_type=F32) + b_ref[0, 0]
        o_ref[...] = y.astype(o_ref.dtype)


def _moe_down(blk_expert, n_used, blk_first, act, w_down, b_down, layer):
    tn = 1024
    nj = D_MODEL // tn
    grid_spec = pltpu.PrefetchScalarGridSpec(
        num_scalar_prefetch=3,
        grid=(nj, MOE_BLOCKS),
        in_specs=[
            pl.BlockSpec((MOE_BM, D_EXPERT), lambda j, b, be, nb, fr: (_used(b, nb), 0)),
            pl.BlockSpec((1, 1, D_EXPERT, tn), lambda j, b, be, nb, fr: (layer, be[_used(b, nb)], 0, j)),
            pl.BlockSpec((1, 1, 1, tn), lambda j, b, be, nb, fr: (layer, be[_used(b, nb)], 0, j)),
        ],
        out_specs=pl.BlockSpec((MOE_BM, tn), lambda j, b, be, nb, fr: (b, j)),
        scratch_shapes=[pltpu.VMEM((D_EXPERT, tn), BF16)],
    )
    return pl.pallas_call(
        _moe_down_kernel,
        grid_spec=grid_spec,
        out_shape=jax.ShapeDtypeStruct((MOE_ROWS, D_MODEL), F32),
        compiler_params=_cparams(("arbitrary", "arbitrary")),
        name="moe_down",
    )(blk_expert, n_used, blk_first, act, w_down, b_down.reshape(DEPTH, N_EXPERTS, 1, D_MODEL))


ROUTE_TM = 512


def _route_kernel(lg_ref, ei_ref, ew_ref, cnt_ref, base):
    i = pl.program_id(0)

    @pl.when(i == 0)
    def _():
        base[...] = jnp.zeros_like(base)

    tm = ROUTE_TM
    lg = lg_ref[...]
    lane = lax.broadcasted_iota(jnp.int32, (tm, LANE), 1)
    vals, idxs = [], []
    for _ in range(TOP_K):
        m = jnp.max(lg, axis=-1, keepdims=True)
        idx = jnp.min(jnp.where(lg == m, lane, LANE), axis=-1, keepdims=True)
        vals.append(m)
        idxs.append(idx)
        lg = jnp.where(lane == idx, -jnp.inf, lg)
    es = [jnp.exp(v - vals[0]) for v in vals]
    inv = 1.0 / functools.reduce(lambda a, b: a + b, es)
    onehot = functools.reduce(lambda a, b: a + b, [jnp.where(lane == idx, 1.0, 0.0) for idx in idxs])
    ri = lax.broadcasted_iota(jnp.int32, (tm, tm), 0)
    ci = lax.broadcasted_iota(jnp.int32, (tm, tm), 1)
    strict_lower = jnp.where(ri > ci, 1.0, 0.0).astype(BF16)
    before = jnp.dot(strict_lower, onehot.astype(BF16), preferred_element_type=F32) + base[...]
    ei = jnp.zeros((tm, LANE), jnp.int32)
    ew = jnp.zeros((tm, LANE), F32)
    for k in range(TOP_K):
        rank = jnp.sum(jnp.where(lane == idxs[k], before, 0.0), axis=-1, keepdims=True)
        ei = jnp.where(lane == k, idxs[k], ei)
        ei = jnp.where(lane == TOP_K + k, rank.astype(jnp.int32), ei)
        ew = jnp.where(lane == k, es[k] * inv, ew)
    ei_ref[...] = ei
    ew_ref[...] = ew
    base[...] = base[...] + jnp.sum(onehot, axis=0, keepdims=True)
    cnt_ref[...] = base[...]


def _route(logits):
    tm = ROUTE_TM
    return pl.pallas_call(
        _route_kernel,
        grid=(N_TOK // tm,),
        in_specs=[pl.BlockSpec((tm, LANE), lambda i: (i, 0))],
        out_specs=[
            pl.BlockSpec((tm, LANE), lambda i: (i, 0)),
            pl.BlockSpec((tm, LANE), lambda i: (i, 0)),
            pl.BlockSpec((1, LANE), lambda i: (0, 0)),
        ],
        out_shape=[
            jax.ShapeDtypeStruct((N_TOK, LANE), jnp.int32),
            jax.ShapeDtypeStruct((N_TOK, LANE), F32),
            jax.ShapeDtypeStruct((1, LANE), F32),
        ],
        scratch_shapes=[pltpu.VMEM((1, LANE), F32)],
        compiler_params=_cparams(("arbitrary",)),
        name="route",
    )(logits)


def _route_layout(ei, counts):
    cnt = counts[0, :N_EXPERTS].astype(jnp.int32)
    padded = (cnt + MOE_BM - 1) // MOE_BM * MOE_BM
    pad_end = jnp.cumsum(padded)
    pad_start = pad_end - padded
    dest = pad_start[ei[:, :TOP_K]] + ei[:, TOP_K:2 * TOP_K]
    blk_expert = jnp.minimum(
        jnp.searchsorted(pad_end, jnp.arange(MOE_BLOCKS) * MOE_BM, side='right'), N_EXPERTS - 1).astype(jnp.int32)
    n_used = (pad_end[-1:] // MOE_BM).astype(jnp.int32)
    blk_first = jnp.concatenate([jnp.ones((1,), jnp.int32),
                                 (blk_expert[1:] != blk_expert[:-1]).astype(jnp.int32)])
    fill_start = (pad_start + cnt).astype(jnp.int32)
    fill_cnt = (padded - cnt).astype(jnp.int32)
    dest3 = dest.astype(jnp.int32).reshape(N_TOK // MOE_TT, 1, MOE_TT * TOP_K)
    return dest3, blk_expert, n_used, blk_first, fill_start, fill_cnt


MOE_TT = 256


def _row_copy(src, dst, sem):
    return pltpu.make_async_copy(src, dst, sem)


def _scatter_kernel(fs_ref, fc_ref, nb_ref, dest_ref, h_hbm, x_hbm, zblk, sem, zsem):
    i = pl.program_id(0)
    t0 = i * MOE_TT

    @pl.when(i == 0)
    def _():
        zblk[...] = jnp.zeros_like(zblk)
        zrow = zblk.at[pl.ds(0, 1)]

        def fill_expert(e, carry):
            def fill_row(r, c):
                _row_copy(zrow, x_hbm.at[pl.ds(fs_ref[e] + r, 1)], zsem).start()
                return c
            lax.fori_loop(0, fc_ref[e], fill_row, 0)

            def wait_row(r, c):
                _row_copy(zrow, x_hbm.at[pl.ds(0, 1)], zsem).wait()
                return c
            lax.fori_loop(0, fc_ref[e], wait_row, 0)
            return carry
        lax.fori_loop(0, N_EXPERTS, fill_expert, 0)

        def fill_block(b, carry):
            cp = _row_copy(zblk, x_hbm.at[pl.ds(pl.multiple_of(b * MOE_BM, MOE_BM), MOE_BM)], zsem)
            cp.start()
            cp.wait()
            return carry
        lax.fori_loop(nb_ref[0], MOE_BLOCKS, fill_block, 0)

    def issue(t, carry):
        for k in range(TOP_K):
            d = dest_ref[0, 0, t * TOP_K + k]
            _row_copy(h_hbm.at[pl.ds(t0 + t, 1)], x_hbm.at[pl.ds(d, 1)], sem).start()
        return carry
    lax.fori_loop(0, MOE_TT, issue, 0, unroll=8)

    def drain(t, carry):
        _row_copy(h_hbm.at[pl.ds(0, 1)], x_hbm.at[pl.ds(0, 1)], sem).wait()
        return carry
    lax.fori_loop(0, MOE_TT * TOP_K, drain, 0, unroll=8)


def _scatter_rows(fill_start, fill_cnt, n_used, dest3, h2):
    grid_spec = pltpu.PrefetchScalarGridSpec(
        num_scalar_prefetch=3,
        grid=(N_TOK // MOE_TT,),
        in_specs=[
            pl.BlockSpec((1, 1, MOE_TT * TOP_K), lambda i, fs, fc, nb: (i, 0, 0), memory_space=pltpu.SMEM),
            pl.BlockSpec(memory_space=pl.ANY),
        ],
        out_specs=pl.BlockSpec(memory_space=pl.ANY),
        scratch_shapes=[
            pltpu.VMEM((MOE_BM, D_MODEL), F32),
            pltpu.SemaphoreType.DMA,
            pltpu.SemaphoreType.DMA,
        ],
    )
    return pl.pallas_call(
        _scatter_kernel,
        grid_spec=grid_spec,
        out_shape=jax.ShapeDtypeStruct((MOE_ROWS, D_MODEL), F32),
        compiler_params=_cparams(("arbitrary",)),
        name="moe_scatter",
    )(fill_start, fill_cnt, n_used, dest3, h2)


def _combine_kernel(dcur_ref, dnext_ref, w_ref, x1_ref, g2_ref, y_hbm, o_ref, buf, sem):
    i = pl.program_id(0)
    n = pl.num_programs(0)
    slot = lax.rem(i, 2)

    def issue(d_ref, s):
        def body(t, carry):
            for k in range(TOP_K):
                d = d_ref[0, 0, t * TOP_K + k]
                _row_copy(y_hbm.at[pl.ds(d, 1)], buf.at[s, k, pl.ds(t, 1)], sem.at[s]).start()
            return carry
        lax.fori_loop(0, MOE_TT, body, 0, unroll=8)

    @pl.when(i == 0)
    def _():
        issue(dcur_ref, 0)

    @pl.when(i + 1 < n)
    def _():
        issue(dnext_ref, 1 - slot)

    def drain(t, carry):
        _row_copy(y_hbm.at[pl.ds(0, 1)], buf.at[slot, 0, pl.ds(0, 1)], sem.at[slot]).wait()
        return carry
    lax.fori_loop(0, MOE_TT * TOP_K, drain, 0, unroll=8)

    acc = None
    for k in range(TOP_K):
        t = buf[slot, k] * w_ref[:, k:k + 1]
        acc = t if acc is None else acc + t
    o_ref[...] = x1_ref[...] + g2_ref[0] * acc


def _combine(dest3, ew, x1, g2, y_rows):
    nt = N_TOK // MOE_TT
    return pl.pallas_call(
        _combine_kernel,
        grid=(nt,),
        in_specs=[
            pl.BlockSpec((1, 1, MOE_TT * TOP_K), lambda i: (i, 0, 0), memory_space=pltpu.SMEM),
            pl.BlockSpec((1, 1, MOE_TT * TOP_K), lambda i: (jnp.minimum(i + 1, nt - 1), 0, 0),
                         memory_space=pltpu.SMEM),
            pl.BlockSpec((MOE_TT, LANE), lambda i: (i, 0)),
            pl.BlockSpec((MOE_TT, D_MODEL), lambda i: (i, 0)),
            pl.BlockSpec((1, 1, D_MODEL), lambda i: (_cond_of_tile(i, MOE_TT), 0, 0)),
            pl.BlockSpec(memory_space=pl.ANY),
        ],
        out_specs=pl.BlockSpec((MOE_TT, D_MODEL), lambda i: (i, 0)),
        out_shape=jax.ShapeDtypeStruct((N_TOK, D_MODEL), F32),
        scratch_shapes=[
            pltpu.VMEM((2, TOP_K, MOE_TT, D_MODEL), F32),
            pltpu.SemaphoreType.DMA((2,)),
        ],
        compiler_params=_cparams(("arbitrary",)),
        name="moe_combine",
    )(dest3, dest3, ew, x1, g2.reshape(N_COND, 1, D_MODEL), y_rows)


def _rope_tables(n):
    t = jnp.arange(n)
    row = (t // GRID_W).astype(F32)
    col = (t % GRID_W).astype(F32)
    quarter = DA_HEADDIM // 4
    inv_freq = ROPE_BASE ** (-jnp.arange(quarter, dtype=F32) / quarter)
    ang_r = row[:, None] * inv_freq
    ang_c = col[:, None] * inv_freq
    ang = jnp.concatenate([ang_r, ang_r, ang_c, ang_c] * 2, axis=-1)
    cos, sin = jnp.cos(ang), jnp.sin(ang)
    lower = (jnp.arange(HEAD_W) % (2 * quarter)) < quarter
    return cos, jnp.where(lower, -sin, 0.0), jnp.where(lower, 0.0, sin)


def _group_mean_matrix():
    g = jnp.arange(HEAD_W) // DA_HEADDIM
    return jnp.where(g[:, None] == g[None, :], 1.0 / DA_HEADDIM, 0.0).astype(BF16)


def _layer(l, x, mod, p, cache_k, cache_v, h0_t):
    sh1, sc1, g1, sh2, sc2, g2 = [mod[l, :N_COND, i * D_MODEL:(i + 1) * D_MODEL] for i in range(6)]
    w_in = p['w_in'][l]
    o_dt = 2 * BRANCH_WIDTH + 2 * SSD_BC
    o_q = o_dt + 2 * SSD_HEADS
    o_u = o_q + 3 * BRANCH_WIDTH
    o_g = o_u + 2 * BRANCH_WIDTH
    wdt = jnp.pad(w_in[:, o_dt:o_q], ((0, 0), (0, LANE - 2 * SSD_HEADS))).astype(BF16)
    h, dt_raw = _prologue(x, p['norm1_g'][l], sc1, sh1, wdt)

    o_x = BRANCH_WIDTH
    z = _matmul(h, w_in[:, :o_x].astype(BF16), 1024, 1024, BF16, "inproj_z")
    xbc = _matmul(h, w_in[:, o_x:o_dt].astype(BF16), 1024, 1280, BF16, "inproj_xbc")
    qkv = _matmul(h, w_in[:, o_q:o_u].astype(BF16), 1024, 1024, BF16, "inproj_attn")
    uv = _matmul(h, w_in[:, o_u:o_g].astype(BF16), 1024, 1024, BF16, "inproj_sgu")
    gates = _matmul(h, w_in[:, o_g:].astype(BF16), 1024, 1024, BF16, "inproj_gates")

    xc = _conv_silu(xbc, p['conv_w'][l], p['conv_b'][l])
    dt_bias = jnp.pad(p['dt_bias'][l].reshape(1, 2 * SSD_HEADS), ((0, 0), (0, LANE - 2 * SSD_HEADS)))
    a_neg = jnp.pad(-jnp.exp(p['a_log'][l].reshape(1, 2 * SSD_HEADS)), ((0, 0), (0, LANE - 2 * SSD_HEADS)))
    y_f, y_b, h_fin = _ssd(xc, dt_raw, dt_bias, a_neg, h0_t[:, l])
    y_a = _ssd_post(y_f, y_b, xc, z, jnp.repeat(p['d_skip'][l], SSD_HEADDIM), p['ssm_norm_g'][l])
    state_l = jnp.swapaxes(h_fin[:BATCH], -1, -2)

    gq = jnp.tile(p['qk_norm_g'][l][0], 2).reshape(1, HEAD_W) * (DA_HEADDIM ** -0.5 * math.log2(math.e))
    gk = jnp.tile(p['qk_norm_g'][l][1], 2).reshape(1, HEAD_W)
    gm = _group_mean_matrix()
    lam_init = 0.8 - 0.6 * math.exp(-0.3 * l)
    lv = p['lam'][l]
    lam = (jnp.exp(jnp.sum(lv[0] * lv[1])) - jnp.exp(jnp.sum(lv[2] * lv[3])) + lam_init).reshape(1)
    qp, kp, k_store, v_store = _attn_pre(qkv, gq, gk, gm, None, row_off=0, nrows=N_PROMPT, store=True,
                                         name="attn_pre_prompt")
    kv_p = [pl.BlockSpec((SEQ, HEAD_W), lambda b, h, i: (b, h)),
            pl.BlockSpec((SEQ, HEAD_W), lambda b, h, i: (b, QKV_V_BLOCK + h))]
    o_p = _attention(lam, qp, [kp], [qkv], kv_p, p['subln_g'][l], nseq=BATCH, lq=SEQ,
                     lam_init=lam_init, name="attn_prompt")
    qs, ks = _attn_pre(qkv, gq, gk, gm, _rope_tables(DEC_SEQ), row_off=N_PROMPT, nrows=N_SAMPLE, store=False,
                       name="attn_pre_sample")
    seq0 = N_PROMPT // DEC_SEQ
    ctx_spec = pl.BlockSpec((None, None, PAST_LEN, HEAD_W), lambda b, h, i: (b, l, 0, h))
    kv_s = [pl.BlockSpec((DEC_SEQ, HEAD_W), lambda b, h, i: (b, h)), ctx_spec,
            pl.BlockSpec((DEC_SEQ, HEAD_W), lambda b, h, i: (seq0 + b, QKV_V_BLOCK + h)), ctx_spec]
    ck = cache_k.reshape(DEC_BATCH, DEPTH, PAST_LEN, BRANCH_WIDTH)
    cv = cache_v.reshape(DEC_BATCH, DEPTH, PAST_LEN, BRANCH_WIDTH)
    o_s = _attention(lam, qs, [ks, ck], [qkv, cv], kv_s, p['subln_g'][l], nseq=DEC_BATCH, lq=DEC_SEQ,
                     lam_init=lam_init, name="attn_sample")
    y_b2 = jnp.concatenate([o_p, o_s], axis=0)
    k_store = k_store.reshape(BATCH, SEQ, DA_HEADS, HEAD_W)
    v_store = v_store.reshape(BATCH, SEQ, DA_HEADS, HEAD_W)

    y_c = _sgu(uv, uv, p['sgu_norm_g'][l], p['w_s'][l], p['b_s'][l].T)

    merged = _merge(y_a, y_b2, y_c, p['w_branch'][l].astype(BF16), gates)
    wr = jnp.pad(p['w_router'][l], ((0, 0), (0, LANE - N_EXPERTS)))
    wr_hi = wr.astype(BF16)
    wr_lo = (wr - wr_hi.astype(F32)).astype(BF16)
    br = jnp.pad(p['b_router'][l].reshape(1, N_EXPERTS), ((0, 0), (0, LANE - N_EXPERTS)),
                 constant_values=-1e30)
    x1, h2, logits = _outproj(merged, p['w_out'][l].astype(BF16), x, g1, p['norm2_g'][l], sc2, sh2,
                              wr_hi, wr_lo, br)

    ei, ew, counts = _route(logits)
    dest3, blk_expert, n_used, blk_first, fill_start, fill_cnt = _route_layout(ei, counts)
    x_rows = _scatter_rows(fill_start, fill_cnt, n_used, dest3, h2)
    act = _moe_gu(blk_expert, n_used, blk_first, x_rows, p['w_gu'], p['b_gu'], l)
    y_rows = _moe_down(blk_expert, n_used, blk_first, act, p['w_down'], p['b_down'], l)
    x2 = _combine(dest3, ew, x1, g2, y_rows)
    return x2, k_store, v_store, state_l


def kernel(x_prompt, x_sample, c, cache_k, cache_v, state_ssm, c_ctx, w_ada, b_ada, norm1_g, norm2_g,
           w_in, conv_w, conv_b, dt_bias, a_log, d_skip, ssm_norm_g, qk_norm_g, lam, subln_g,
           sgu_norm_g, w_s, b_s, w_branch, w_out, w_router, b_router, w_gu, b_gu, w_down, b_down):
    p = {'norm1_g': norm1_g, 'norm2_g': norm2_g, 'w_in': w_in, 'conv_w': conv_w, 'conv_b': conv_b,
         'dt_bias': dt_bias, 'a_log': a_log, 'd_skip': d_skip, 'ssm_norm_g': ssm_norm_g,
         'qk_norm_g': qk_norm_g, 'lam': lam, 'subln_g': subln_g, 'sgu_norm_g': sgu_norm_g, 'w_s': w_s,
         'b_s': b_s, 'w_branch': w_branch, 'w_out': w_out, 'w_router': w_router, 'b_router': b_router,
         'w_gu': w_gu, 'b_gu': b_gu, 'w_down': w_down, 'b_down': b_down}
    cond8 = jnp.concatenate([c_ctx[None, :], c, jnp.zeros((SUBLANE - N_COND, D_MODEL), F32)], axis=0)
    mod = _ada_all(cond8, w_ada, b_ada)
    h0_t = jnp.concatenate([jnp.zeros((1,) + state_ssm.shape[1:], F32), state_ssm], axis=0)
    h0_t = jnp.swapaxes(h0_t, -1, -2)
    x = jnp.concatenate([x_prompt.reshape(N_PROMPT, D_MODEL), x_sample.reshape(N_SAMPLE, D_MODEL)], axis=0)
    ks, vs, ss = [], [], []
    for l in range(DEPTH):
        x, k_l, v_l, s_l = _layer(l, x, mod, p, cache_k, cache_v, h0_t)
        ks.append(k_l)
        vs.append(v_l)
        ss.append(s_l)
    y_prompt = x[:N_PROMPT].reshape(BATCH, SEQ, D_MODEL)
    y_sample = x[N_PROMPT:].reshape(DEC_BATCH, DEC_SEQ, D_MODEL)
    return (y_prompt, y_sample, jnp.stack(ks, axis=1), jnp.stack(vs, axis=1), jnp.stack(ss, axis=1))
```

```python
import functools
import math

import jax
import jax.numpy as jnp
import numpy as np
from jax import lax
from jax.experimental import pallas as pl
from jax.experimental.pallas import tpu as pltpu

F32 = jnp.float32
BF16 = jnp.bfloat16

D_MODEL = 2048
BATCH = 32
SEQ = 256
DEPTH = 2
DEC_BATCH = 4
DEC_SEQ = 2048
PAST_LEN = 512
GRID_W = 64
BRANCH_WIDTH = 1024
SSD_HEADDIM = 64
SSD_HEADS = 16
SSD_GROUPS = 2
SSD_STATE = 64
SSD_CONV = 7
SSD_CHUNK = 128
SSD_BC = 128
DA_HEADS = 8
DA_HEADDIM = 64
ROPE_BASE = 10000.0
SG_CHUNK = 128
SG_GROUPS = 8
N_EXPERTS = 32
TOP_K = 4
D_EXPERT = 2048
SWIGLU_LIMIT = 7.0
SWIGLU_ALPHA = 1.702
NORM_EPS = 1e-6

N_PROMPT = BATCH * SEQ
N_SAMPLE = DEC_BATCH * DEC_SEQ
N_TOK = N_PROMPT + N_SAMPLE
N_COND = 1 + DEC_BATCH

LANE = 128
SUBLANE = 8
VMEM_LIMIT = 56 * 1024 * 1024

MOE_BM = 512
MOE_BLOCKS = N_TOK * TOP_K // MOE_BM + N_EXPERTS
MOE_ROWS = MOE_BLOCKS * MOE_BM


def _cparams(sem):
    return pltpu.CompilerParams(dimension_semantics=sem, vmem_limit_bytes=VMEM_LIMIT)


def _cond_of_tile(i, tm):
    row = i * tm
    return jnp.where(row < N_PROMPT, 0, 1 + (row - N_PROMPT) // DEC_SEQ)


def _ada_kernel(c_ref, w_ref, b_ref, o_ref):
    c = c_ref[...]
    s = (c * jax.nn.sigmoid(c)).astype(BF16)
    o_ref[0] = jnp.dot(s, w_ref[0].astype(BF16), preferred_element_type=F32) + b_ref[0]


def _ada_all(cond8, w_ada, b_ada):
    tn = 1024
    n = 6 * D_MODEL
    return pl.pallas_call(
        _ada_kernel,
        grid=(DEPTH, n // tn),
        in_specs=[
            pl.BlockSpec((SUBLANE, D_MODEL), lambda l, j: (0, 0)),
            pl.BlockSpec((1, D_MODEL, tn), lambda l, j: (l, 0, j)),
            pl.BlockSpec((1, 1, tn), lambda l, j: (l, 0, j)),
        ],
        out_specs=pl.BlockSpec((1, SUBLANE, tn), lambda l, j: (l, 0, j)),
        out_shape=jax.ShapeDtypeStruct((DEPTH, SUBLANE, n), F32),
        compiler_params=_cparams(("arbitrary", "arbitrary")),
        name="ada",
    )(cond8, w_ada, b_ada.reshape(DEPTH, 1, n))


def _prologue_kernel(x_ref, g_ref, sc_ref, sh_ref, wdt_ref, h_ref, dt_ref):
    x = x_ref[...]
    ms = jnp.mean(x * x, axis=-1, keepdims=True)
    y = x * lax.rsqrt(ms + NORM_EPS) * g_ref[...]
    h = (y * (1.0 + sc_ref[0]) + sh_ref[0]).astype(BF16)
    h_ref[...] = h
    dt_ref[...] = jnp.dot(h, wdt_ref[...], preferred_element_type=F32)


def _prologue(x, g, sc, sh, wdt):
    tm = 512
    return pl.pallas_call(
        _prologue_kernel,
        grid=(N_TOK // tm,),
        in_specs=[
            pl.BlockSpec((tm, D_MODEL), lambda i: (i, 0)),
            pl.BlockSpec((1, D_MODEL), lambda i: (0, 0)),
            pl.BlockSpec((1, 1, D_MODEL), lambda i: (_cond_of_tile(i, tm), 0, 0)),
            pl.BlockSpec((1, 1, D_MODEL), lambda i: (_cond_of_tile(i, tm), 0, 0)),
            pl.BlockSpec((D_MODEL, LANE), lambda i: (0, 0)),
        ],
        out_specs=[
            pl.BlockSpec((tm, D_MODEL), lambda i: (i, 0)),
            pl.BlockSpec((tm, LANE), lambda i: (i, 0)),
        ],
        out_shape=[
            jax.ShapeDtypeStruct((N_TOK, D_MODEL), BF16),
            jax.ShapeDtypeStruct((N_TOK, LANE), F32),
        ],
        compiler_params=_cparams(("arbitrary",)),
        name="prologue",
    )(x, g.reshape(1, D_MODEL), sc.reshape(N_COND, 1, D_MODEL), sh.reshape(N_COND, 1, D_MODEL), wdt)


def _mm_kernel(a_ref, w_ref, o_ref):
    o_ref[...] = jnp.dot(a_ref[...], w_ref[...], preferred_element_type=F32).astype(o_ref.dtype)


def _matmul(a, w, tm, tn, out_dtype, name):
    m, k = a.shape
    n = w.shape[1]
    return pl.pallas_call(
        _mm_kernel,
        grid=(m // tm, n // tn),
        in_specs=[
            pl.BlockSpec((tm, k), lambda i, j: (i, 0)),
            pl.BlockSpec((k, tn), lambda i, j: (0, j)),
        ],
        out_specs=pl.BlockSpec((tm, tn), lambda i, j: (i, j)),
        out_shape=jax.ShapeDtypeStruct((m, n), out_dtype),
        compiler_params=_cparams(("arbitrary", "arbitrary")),
        name=name,
    )(a, w)


CONV_TM = 256
CONV_HALO = 16


def _conv_kernel(x_ref, prev_ref, next_ref, w_ref, b_ref, o_ref, pad_ref):
    i = pl.program_id(0)
    n_prompt_tiles = N_PROMPT // CONV_TM
    tiles_per_seq = DEC_SEQ // CONV_TM
    j = lax.rem(jnp.maximum(i - n_prompt_tiles, 0), tiles_per_seq)
    is_prompt = i < n_prompt_tiles
    first = jnp.logical_or(is_prompt, j == 0)
    last = jnp.logical_or(is_prompt, j == tiles_per_seq - 1)
    pad_ref[0:CONV_HALO, :] = jnp.where(first, 0.0, prev_ref[...].astype(F32))
    pad_ref[CONV_HALO:CONV_HALO + CONV_TM, :] = x_ref[...].astype(F32)
    pad_ref[CONV_HALO + CONV_TM:, :] = jnp.where(last, 0.0, next_ref[...].astype(F32))
    acc = jnp.broadcast_to(b_ref[...], (CONV_TM, b_ref.shape[1]))
    half = (SSD_CONV - 1) // 2
    for k in range(SSD_CONV):
        acc = acc + pad_ref[pl.ds(CONV_HALO - half + k, CONV_TM), :] * w_ref[k:k + 1, :]
    o_ref[...] = (acc * jax.nn.sigmoid(acc)).astype(o_ref.dtype)


def _conv_silu(xbc, w, b):
    wx = xbc.shape[1]
    hb = CONV_TM // CONV_HALO
    last_halo = N_TOK // CONV_HALO - 1
    return pl.pallas_call(
        _conv_kernel,
        grid=(N_TOK // CONV_TM,),
        in_specs=[
            pl.BlockSpec((CONV_TM, wx), lambda i: (i, 0)),
            pl.BlockSpec((CONV_HALO, wx), lambda i: (jnp.maximum(i * hb - 1, 0), 0)),
            pl.BlockSpec((CONV_HALO, wx), lambda i: (jnp.minimum((i + 1) * hb, last_halo), 0)),
            pl.BlockSpec((SSD_CONV, wx), lambda i: (0, 0)),
            pl.BlockSpec((1, wx), lambda i: (0, 0)),
        ],
        out_specs=pl.BlockSpec((CONV_TM, wx), lambda i: (i, 0)),
        out_shape=jax.ShapeDtypeStruct((N_TOK, wx), BF16),
        scratch_shapes=[pltpu.VMEM((CONV_TM + 2 * CONV_HALO, wx), F32)],
        compiler_params=_cparams(("arbitrary",)),
        name="ssd_conv",
    )(xbc, xbc, xbc, w, b.reshape(1, wx))


def _ssd_post_kernel(yf_ref, yb_ref, xs_ref, z_ref, d_ref, g_ref, o_ref):
    z = z_ref[...].astype(F32)
    y = yf_ref[...] + yb_ref[...] + d_ref[...] * xs_ref[...].astype(F32)
    y = y * (z * jax.nn.sigmoid(z))
    y = y * lax.rsqrt(jnp.mean(y * y, axis=-1, keepdims=True) + NORM_EPS) * g_ref[...]
    o_ref[...] = y.astype(o_ref.dtype)


def _ssd_post(y_f, y_b, xc, z, d_full, g):
    tm = 512
    row = lambda i: (i, 0)
    fixed = lambda i: (0, 0)
    return pl.pallas_call(
        _ssd_post_kernel,
        grid=(N_TOK // tm,),
        in_specs=[
            pl.BlockSpec((tm, BRANCH_WIDTH), row),
            pl.BlockSpec((tm, BRANCH_WIDTH), row),
            pl.BlockSpec((tm, BRANCH_WIDTH), row),
            pl.BlockSpec((tm, BRANCH_WIDTH), row),
            pl.BlockSpec((1, BRANCH_WIDTH), fixed),
            pl.BlockSpec((1, BRANCH_WIDTH), fixed),
        ],
        out_specs=pl.BlockSpec((tm, BRANCH_WIDTH), row),
        out_shape=jax.ShapeDtypeStruct((N_TOK, BRANCH_WIDTH), BF16),
        compiler_params=_cparams(("arbitrary",)),
        name="ssd_post",
    )(y_f, y_b, xc, z, d_full.reshape(1, BRANCH_WIDTH), g.reshape(1, BRANCH_WIDTH))


def _split3(x):
    hi = x.astype(BF16)
    r = x - hi.astype(F32)
    mid = r.astype(BF16)
    lo = (r - mid.astype(F32)).astype(BF16)
    return hi, mid, lo


def _ssd_kernel(fwd_ref, bwd_ref, first_ref, seq_ref, h0i_ref,
                xf_ref, xb_ref, dtf_ref, dtb_ref, bias_ref, a_ref, h0_ref,
                yf_ref, yb_ref, hout_ref, hstate):
    s = pl.program_id(0)
    q = SSD_CHUNK

    @pl.when(first_ref[s] == 1)
    def _():
        hstate[...] = h0_ref[0]

    ri = lax.broadcasted_iota(jnp.int32, (q, q), 0)
    ci = lax.broadcasted_iota(jnp.int32, (q, q), 1)

    for d in (0, 1):
        x_ref, dt_ref, y_ref = ((xf_ref, dtf_ref, yf_ref), (xb_ref, dtb_ref, yb_ref))[d]
        xbc = x_ref[...].astype(F32)
        dt = jax.nn.softplus(dt_ref[...] + bias_ref[...])
        adt = dt * a_ref[...]
        mask = (ri >= ci) if d == 0 else (ri <= ci)
        tri = jnp.where(mask, 1.0, 0.0).astype(BF16)
        hi, mid, lo = _split3(adt)
        cum = (jnp.dot(tri, hi, preferred_element_type=F32)
               + jnp.dot(tri, mid, preferred_element_type=F32)
               + jnp.dot(tri, lo, preferred_element_type=F32))
        cum_t = cum.T
        dt_t = dt.T
        tot = cum[q - 1:q, :] if d == 0 else cum[0:1, :]
        bm = xbc[:, 1024:1152]
        cm = xbc[:, 1152:1280]
        bm_t = bm.T
        ys = []
        for g in range(SSD_GROUPS):
            cg = cm[:, g * SSD_STATE:(g + 1) * SSD_STATE]
            bg = bm[:, g * SSD_STATE:(g + 1) * SSD_STATE]
            cb = lax.dot_general(cg, bg, (((1,), (1,)), ((), ())), preferred_element_type=F32)
            bg_t = bm_t[g * SSD_STATE:(g + 1) * SSD_STATE, :]
            for hh in range(SSD_HEADS // SSD_GROUPS):
                h = g * (SSD_HEADS // SSD_GROUPS) + hh
                ch = d * SSD_HEADS + h
                pcol = cum[:, ch:ch + 1]
                prow = cum_t[ch:ch + 1, :]
                dtrow = dt_t[ch:ch + 1, :]
                lm = jnp.exp(jnp.where(mask, pcol - prow, -1e30))
                scores = cb * lm * dtrow
                xs_h = xbc[:, h * SSD_HEADDIM:(h + 1) * SSD_HEADDIM]
                h_t = hstate[d, h]
                y = jnp.dot(scores, xs_h, preferred_element_type=F32)
                y = y + jnp.dot(cg, h_t, preferred_element_type=F32) * jnp.exp(pcol)
                tot_h = tot[:, ch:ch + 1]
                w_row = dtrow * jnp.exp(tot_h - prow)
                hstate[d, h] = jnp.exp(tot_h) * h_t + jnp.dot(bg_t * w_row, xs_h, preferred_element_type=F32)
                ys.append(y)
        y_ref[...] = jnp.concatenate(ys, axis=1)
    hout_ref[0] = hstate[...]


def _ssd_tables():
    fwd, bwd, first, seq, h0i = [], [], [], [], []
    cpr = SEQ // SSD_CHUNK
    for b in range(BATCH):
        for c in range(cpr):
            fwd.append(b * cpr + c)
            bwd.append(b * cpr + (cpr - 1 - c))
            first.append(1 if c == 0 else 0)
            seq.append(b)
            h0i.append(0)
    base = N_PROMPT // SSD_CHUNK
    cps = DEC_SEQ // SSD_CHUNK
    for b in range(DEC_BATCH):
        for c in range(cps):
            fwd.append(base + b * cps + c)
            bwd.append(base + b * cps + (cps - 1 - c))
            first.append(1 if c == 0 else 0)
            seq.append(BATCH + b)
            h0i.append(1 + b)
    return [jnp.asarray(np.asarray(t, np.int32)) for t in (fwd, bwd, first, seq, h0i)]


def _ssd(xc, dt_raw, dt_bias128, a128, h0_t):
    q = SSD_CHUNK
    nsteps = N_TOK // q
    wx = BRANCH_WIDTH + 2 * SSD_BC
    st_shape = (2, SSD_HEADS, SSD_STATE, SSD_HEADDIM)
    grid_spec = pltpu.PrefetchScalarGridSpec(
        num_scalar_prefetch=5,
        grid=(nsteps,),
        in_specs=[
            pl.BlockSpec((q, wx), lambda s, f, b, fi, sq, hi: (f[s], 0)),
            pl.BlockSpec((q, wx), lambda s, f, b, fi, sq, hi: (b[s], 0)),
            pl.BlockSpec((q, LANE), lambda s, f, b, fi, sq, hi: (f[s], 0)),
            pl.BlockSpec((q, LANE), lambda s, f, b, fi, sq, hi: (b[s], 0)),
            pl.BlockSpec((1, LANE), lambda s, f, b, fi, sq, hi: (0, 0)),
            pl.BlockSpec((1, LANE), lambda s, f, b, fi, sq, hi: (0, 0)),
            pl.BlockSpec((1,) + st_shape, lambda s, f, b, fi, sq, hi: (hi[s], 0, 0, 0, 0)),
        ],
        out_specs=[
            pl.BlockSpec((q, BRANCH_WIDTH), lambda s, f, b, fi, sq, hi: (f[s], 0)),
            pl.BlockSpec((q, BRANCH_WIDTH), lambda s, f, b, fi, sq, hi: (b[s], 0)),
            pl.BlockSpec((1,) + st_shape, lambda s, f, b, fi, sq, hi: (sq[s], 0, 0, 0, 0)),
        ],
        scratch_shapes=[pltpu.VMEM(st_shape, F32)],
    )
    return pl.pallas_call(
        _ssd_kernel,
        grid_spec=grid_spec,
        out_shape=[
            jax.ShapeDtypeStruct((N_TOK, BRANCH_WIDTH), F32),
            jax.ShapeDtypeStruct((N_TOK, BRANCH_WIDTH), F32),
            jax.ShapeDtypeStruct((BATCH + DEC_BATCH,) + st_shape, F32),
        ],
        compiler_params=_cparams(("arbitrary",)),
        name="ssd",
    )(*_ssd_tables(), xc, xc, dt_raw, dt_raw, dt_bias128, a128, h0_t)


HEAD_W = 2 * DA_HEADDIM
QKV_V_BLOCK = 2 * DA_HEADS
ATTN_PRE_TM = 512


def _attn_pre_kernel(*refs, rope, store):
    q_ref, k_ref, v_ref, gq_ref, gk_ref, gm_ref = refs[:6]
    refs = refs[6:]
    if rope:
        cos_ref, sa_ref, sb_ref = refs[:3]
        refs = refs[3:]
    qo_ref, ko_ref = refs[:2]
    if store:
        ks_ref, vs_ref = refs[2:4]
        vs_ref[...] = v_ref[...].astype(F32)
    gm = gm_ref[...]
    for x_ref, g_ref, o_ref, is_k in ((q_ref, gq_ref, qo_ref, False), (k_ref, gk_ref, ko_ref, True)):
        for s in range(DA_HEADS):
            cols = slice(s * HEAD_W, (s + 1) * HEAD_W)
            xs = x_ref[:, cols].astype(F32)
            xx = xs * xs
            hi = xx.astype(BF16)
            lo = (xx - hi.astype(F32)).astype(BF16)
            ms = jnp.dot(hi, gm, preferred_element_type=F32) + jnp.dot(lo, gm, preferred_element_type=F32)
            xn = xs * lax.rsqrt(ms + NORM_EPS) * g_ref[...]
            if store and is_k:
                ks_ref[:, cols] = xn
            if rope:
                xn = (xn * cos_ref[...] + pltpu.roll(xn, HEAD_W - 16, 1) * sa_ref[...]
                      + pltpu.roll(xn, 16, 1) * sb_ref[...])
            o_ref[:, cols] = xn.astype(o_ref.dtype)


def _attn_pre(qkv, gq, gk, gm, tables, *, row_off, nrows, store, name):
    tm = ATTN_PRE_TM
    off = row_off // tm
    rope = tables is not None
    in_specs = [
        pl.BlockSpec((tm, BRANCH_WIDTH), lambda i: (i + off, 0)),
        pl.BlockSpec((tm, BRANCH_WIDTH), lambda i: (i + off, 1)),
        pl.BlockSpec((tm, BRANCH_WIDTH), lambda i: (i + off, 2)),
        pl.BlockSpec((1, HEAD_W), lambda i: (0, 0)),
        pl.BlockSpec((1, HEAD_W), lambda i: (0, 0)),
        pl.BlockSpec((HEAD_W, HEAD_W), lambda i: (0, 0)),
    ]
    args = [qkv, qkv, qkv, gq, gk, gm]
    if rope:
        tps = DEC_SEQ // tm
        in_specs += [pl.BlockSpec((tm, HEAD_W), lambda i: (lax.rem(i, tps), 0))] * 3
        args += list(tables)
    out_specs = [pl.BlockSpec((tm, BRANCH_WIDTH), lambda i: (i, 0))] * 2
    out_shape = [jax.ShapeDtypeStruct((nrows, BRANCH_WIDTH), BF16)] * 2
    if store:
        out_specs += [pl.BlockSpec((tm, BRANCH_WIDTH), lambda i: (i, 0))] * 2
        out_shape += [jax.ShapeDtypeStruct((nrows, BRANCH_WIDTH), F32)] * 2
    return pl.pallas_call(
        functools.partial(_attn_pre_kernel, rope=rope, store=store),
        grid=(nrows // tm,),
        in_specs=in_specs,
        out_specs=out_specs,
        out_shape=out_shape,
        compiler_params=_cparams(("arbitrary",)),
        name=name,
    )(*args)


def _attn_kernel(lam_ref, q_ref, *refs, lam_init, nkv):
    k_refs = refs[:nkv]
    v_refs = refs[nkv:2 * nkv]
    g_ref, o_ref = refs[2 * nkv:]
    lam = lam_ref[0]
    qb = q_ref[...]
    ks = [k_ref[...].astype(BF16) for k_ref in k_refs]
    es, coef = [], []
    for m in (0, 1):
        cols = slice(m * DA_HEADDIM, (m + 1) * DA_HEADDIM)
        ss = [lax.dot_general(qb[:, cols], kb[:, cols], (((1,), (1,)), ((), ())),
                              preferred_element_type=F32) for kb in ks]
        mx = functools.reduce(jnp.maximum, [jnp.max(s, axis=-1, keepdims=True) for s in ss])
        em = [jnp.exp2(s - mx) for s in ss]
        tot = functools.reduce(lambda a, b: a + b, [jnp.sum(e, axis=-1, keepdims=True) for e in em])
        es.append(em)
        coef.append(1.0 / tot)
    o = None
    for i in range(nkv):
        w = es[0][i] * coef[0] - es[1][i] * (lam * coef[1])
        t = jnp.dot(w.astype(BF16), v_refs[i][...].astype(BF16), preferred_element_type=F32)
        o = t if o is None else o + t
    o = o * lax.rsqrt(jnp.mean(o * o, axis=-1, keepdims=True) + NORM_EPS) * g_ref[...]
    o_ref[...] = (o * (1.0 - lam_init)).astype(o_ref.dtype)


def _attention(lam, q, ks, vs, kv_specs, subln_g, *, nseq, lq, lam_init, name):
    tq = 256
    nqb = lq // tq
    nkv = len(ks)
    grid_spec = pltpu.PrefetchScalarGridSpec(
        num_scalar_prefetch=0,
        grid=(nseq, DA_HEADS, nqb),
        in_specs=[
            pl.BlockSpec(memory_space=pltpu.SMEM),
            pl.BlockSpec((tq, HEAD_W), lambda b, h, i: (b * nqb + i, h)),
            *kv_specs,
            pl.BlockSpec((1, HEAD_W), lambda b, h, i: (0, 0)),
        ],
        out_specs=pl.BlockSpec((tq, HEAD_W), lambda b, h, i: (b * nqb + i, h)),
    )
    return pl.pallas_call(
        functools.partial(_attn_kernel, lam_init=lam_init, nkv=nkv),
        grid_spec=grid_spec,
        out_shape=jax.ShapeDtypeStruct((nseq * lq, DA_HEADS * HEAD_W), BF16),
        compiler_params=_cparams(("arbitrary", "arbitrary", "arbitrary")),
        name=name,
    )(lam, q, *ks, *vs, subln_g.reshape(1, HEAD_W))


def _sgu_kernel(u_ref, v_ref, g_ref, ws_ref, bt_ref, o_ref, *, chunks):
    v = v_ref[...].astype(F32)
    mu = jnp.mean(v, axis=-1, keepdims=True)
    vc = v - mu
    var = jnp.mean(vc * vc, axis=-1, keepdims=True)
    vn = (vc * lax.rsqrt(var + NORM_EPS) * g_ref[...]).astype(BF16)
    u = u_ref[...].astype(F32)
    gw = BRANCH_WIDTH // SG_GROUPS
    for c in range(chunks):
        rows = slice(c * SG_CHUNK, (c + 1) * SG_CHUNK)
        outs = []
        for g in range(SG_GROUPS):
            cols = slice(g * gw, (g + 1) * gw)
            mixed = jnp.dot(ws_ref[g].astype(BF16), vn[rows, cols], preferred_element_type=F32)
            mixed = mixed + bt_ref[:, g:g + 1]
            outs.append(u[rows, cols] * mixed)
        o_ref[rows, :] = jnp.concatenate(outs, axis=1).astype(o_ref.dtype)


def _sgu(u, v, g, ws, bs_t):
    chunks = 4
    tm = chunks * SG_CHUNK
    return pl.pallas_call(
        functools.partial(_sgu_kernel, chunks=chunks),
        grid=(N_TOK // tm,),
        in_specs=[
            pl.BlockSpec((tm, BRANCH_WIDTH), lambda i: (i, 0)),
            pl.BlockSpec((tm, BRANCH_WIDTH), lambda i: (i, 1)),
            pl.BlockSpec((1, BRANCH_WIDTH), lambda i: (0, 0)),
            pl.BlockSpec((SG_GROUPS, SG_CHUNK, SG_CHUNK), lambda i: (0, 0, 0)),
            pl.BlockSpec((SG_CHUNK, SG_GROUPS), lambda i: (0, 0)),
        ],
        out_specs=pl.BlockSpec((tm, BRANCH_WIDTH), lambda i: (i, 0)),
        out_shape=jax.ShapeDtypeStruct((N_TOK, BRANCH_WIDTH), BF16),
        compiler_params=_cparams(("arbitrary",)),
        name="sgu",
    )(u, v, g.reshape(1, BRANCH_WIDTH), ws, bs_t)


def _merge_kernel(ya_ref, yb_ref, yc_ref, wb_ref, ga_ref, gb_ref, gc_ref, o_ref):
    acc = None
    for k, (y_ref, g_ref) in enumerate(((ya_ref, ga_ref), (yb_ref, gb_ref), (yc_ref, gc_ref))):
        t = jnp.dot(y_ref[...], wb_ref[k], preferred_element_type=F32)
        t = t * jax.nn.sigmoid(g_ref[...].astype(F32))
        acc = t if acc is None else acc + t
    o_ref[...] = acc.astype(o_ref.dtype)


def _merge(ya, yb, yc, wb, gates):
    tm, tn = 1024, 1024
    nj = D_MODEL // tn
    yspec = pl.BlockSpec((tm, BRANCH_WIDTH), lambda i, j: (i, 0))
    return pl.pallas_call(
        _merge_kernel,
        grid=(N_TOK // tm, nj),
        in_specs=[
            yspec, yspec, yspec,
            pl.BlockSpec((3, BRANCH_WIDTH, tn), lambda i, j: (0, 0, j)),
            pl.BlockSpec((tm, tn), lambda i, j: (i, j)),
            pl.BlockSpec((tm, tn), lambda i, j: (i, nj + j)),
            pl.BlockSpec((tm, tn), lambda i, j: (i, 2 * nj + j)),
        ],
        out_specs=pl.BlockSpec((tm, tn), lambda i, j: (i, j)),
        out_shape=jax.ShapeDtypeStruct((N_TOK, D_MODEL), BF16),
        compiler_params=_cparams(("arbitrary", "arbitrary")),
        name="merge",
    )(ya, yb, yc, wb, gates, gates, gates)


HALF_D = D_MODEL // 2


def _pack_halves(hi_half, lo_half):
    top = lax.bitcast_convert_type(hi_half, jnp.uint32)
    bot = lax.shift_right_logical(lax.bitcast_convert_type(lo_half, jnp.uint32), jnp.uint32(16))
    return top | bot


def _pack_row_bf16(x):
    xb = x.astype(BF16).astype(F32)
    return _pack_halves(xb[:, :HALF_D], xb[:, HALF_D:])


def _unpack_row_bf16(p):
    a = lax.bitcast_convert_type(p & jnp.uint32(0xFFFF0000), F32)
    b = lax.bitcast_convert_type(lax.shift_left(p, jnp.uint32(16)), F32)
    return a, b


def _outproj_kernel(m_ref, w_ref, x_ref, g1_ref, n2_ref, sc_ref, sh_ref, wrh_ref, wrl_ref, br_ref,
                    x1_ref, h2_ref, lg_ref):
    out = jnp.dot(m_ref[...], w_ref[...], preferred_element_type=F32)
    x1 = x_ref[...] + g1_ref[0] * out
    x1_ref[...] = x1
    ms = jnp.mean(x1 * x1, axis=-1, keepdims=True)
    y = x1 * lax.rsqrt(ms + NORM_EPS) * n2_ref[...]
    h2 = y * (1.0 + sc_ref[0]) + sh_ref[0]
    hi = h2.astype(BF16)
    lo = (h2 - hi.astype(F32)).astype(BF16)
    h2_ref[...] = _pack_row_bf16(h2)
    wh = wrh_ref[...]
    lg = (jnp.dot(hi, wh, preferred_element_type=F32) + jnp.dot(lo, wh, preferred_element_type=F32)
          + jnp.dot(hi, wrl_ref[...], preferred_element_type=F32))
    lg_ref[...] = lg + br_ref[...]


def _outproj(merged, w_out, x, g1, n2, sc2, sh2, wr_hi, wr_lo, br):
    tm = 512
    cond = lambda i: (_cond_of_tile(i, tm), 0, 0)
    row = lambda i: (i, 0)
    fixed = lambda i: (0, 0)
    return pl.pallas_call(
        _outproj_kernel,
        grid=(N_TOK // tm,),
        in_specs=[
            pl.BlockSpec((tm, D_MODEL), row),
            pl.BlockSpec((D_MODEL, D_MODEL), fixed),
            pl.BlockSpec((tm, D_MODEL), row),
            pl.BlockSpec((1, 1, D_MODEL), cond),
            pl.BlockSpec((1, D_MODEL), fixed),
            pl.BlockSpec((1, 1, D_MODEL), cond),
            pl.BlockSpec((1, 1, D_MODEL), cond),
            pl.BlockSpec((D_MODEL, LANE), fixed),
            pl.BlockSpec((D_MODEL, LANE), fixed),
            pl.BlockSpec((1, LANE), fixed),
        ],
        out_specs=[
            pl.BlockSpec((tm, D_MODEL), row),
            pl.BlockSpec((tm, HALF_D), row),
            pl.BlockSpec((tm, LANE), row),
        ],
        out_shape=[
            jax.ShapeDtypeStruct((N_TOK, D_MODEL), F32),
            jax.ShapeDtypeStruct((N_TOK, HALF_D), jnp.uint32),
            jax.ShapeDtypeStruct((N_TOK, LANE), F32),
        ],
        compiler_params=_cparams(("arbitrary",)),
        name="outproj",
    )(merged, w_out, x, g1.reshape(N_COND, 1, D_MODEL), n2.reshape(1, D_MODEL),
      sc2.reshape(N_COND, 1, D_MODEL), sh2.reshape(N_COND, 1, D_MODEL), wr_hi, wr_lo, br)


def _used(b, nb):
    return jnp.minimum(b, nb[0] - 1)


MOE_HALF = MOE_BM // 2


def _moe_gu_kernel(be_ref, nb_ref, fr_ref, nv_ref, x_ref, wg_ref, wu_ref, bg_ref, bu_ref, o_ref, wg_bf, wu_bf):
    b = pl.program_id(1)
    live = b < nb_ref[0]

    @pl.when(jnp.logical_not(live))
    def _():
        o_ref[...] = jnp.zeros_like(o_ref)

    @pl.when(jnp.logical_and(live, fr_ref[b] == 1))
    def _():
        wg_bf[...] = wg_ref[0, 0].astype(BF16)
        wu_bf[...] = wu_ref[0, 0].astype(BF16)

    def half(r0):
        rows = pl.ds(r0, MOE_HALF)
        xa, xb = _unpack_row_bf16(x_ref[rows, :])
        xa = xa.astype(BF16)
        xb = xb.astype(BF16)
        gate = (jnp.dot(xa, wg_bf[:HALF_D, :], preferred_element_type=F32)
                + jnp.dot(xb, wg_bf[HALF_D:, :], preferred_element_type=F32) + bg_ref[0, 0])
        up = (jnp.dot(xa, wu_bf[:HALF_D, :], preferred_element_type=F32)
              + jnp.dot(xb, wu_bf[HALF_D:, :], preferred_element_type=F32) + bu_ref[0, 0])
        gate = jnp.minimum(gate, SWIGLU_LIMIT)
        up = jnp.clip(up, -SWIGLU_LIMIT, SWIGLU_LIMIT)
        act = (up + 1.0) * gate * jax.nn.sigmoid(SWIGLU_ALPHA * gate)
        o_ref[rows, :] = act.astype(o_ref.dtype)

    @pl.when(live)
    def _():
        half(0)

    @pl.when(jnp.logical_and(live, nv_ref[b] > MOE_HALF))
    def _():
        half(MOE_HALF)

    @pl.when(jnp.logical_and(live, nv_ref[b] <= MOE_HALF))
    def _():
        o_ref[pl.ds(MOE_HALF, MOE_HALF), :] = jnp.zeros((MOE_HALF, o_ref.shape[1]), o_ref.dtype)


def _moe_gu(blk_expert, n_used, blk_first, blk_rows, x_rows, w_gu, b_gu, layer):
    tn = 1024
    nj = D_EXPERT // tn
    grid_spec = pltpu.PrefetchScalarGridSpec(
        num_scalar_prefetch=4,
        grid=(nj, MOE_BLOCKS),
        in_specs=[
            pl.BlockSpec((MOE_BM, HALF_D), lambda j, b, be, nb, fr, nv: (_used(b, nb), 0)),
            pl.BlockSpec((1, 1, D_MODEL, tn), lambda j, b, be, nb, fr, nv: (layer, be[_used(b, nb)], 0, j)),
            pl.BlockSpec((1, 1, D_MODEL, tn), lambda j, b, be, nb, fr, nv: (layer, be[_used(b, nb)], 0, nj + j)),
            pl.BlockSpec((1, 1, 1, tn), lambda j, b, be, nb, fr, nv: (layer, be[_used(b, nb)], 0, j)),
            pl.BlockSpec((1, 1, 1, tn), lambda j, b, be, nb, fr, nv: (layer, be[_used(b, nb)], 0, nj + j)),
        ],
        out_specs=pl.BlockSpec((MOE_BM, tn), lambda j, b, be, nb, fr, nv: (b, j)),
        scratch_shapes=[pltpu.VMEM((D_MODEL, tn), BF16), pltpu.VMEM((D_MODEL, tn), BF16)],
    )
    b4 = b_gu.reshape(DEPTH, N_EXPERTS, 1, 2 * D_EXPERT)
    return pl.pallas_call(
        _moe_gu_kernel,
        grid_spec=grid_spec,
        out_shape=jax.ShapeDtypeStruct((MOE_ROWS, D_EXPERT), BF16),
        compiler_params=_cparams(("arbitrary", "arbitrary")),
        name="moe_gu",
    )(blk_expert, n_used, blk_first, blk_rows, x_rows, w_gu, w_gu, b4, b4)


def _moe_down_kernel(be_ref, nb_ref, fr_ref, nv_ref, a_ref, w_ref, b_ref, o_ref, w_bf):
    b = pl.program_id(0)
    live = b < nb_ref[0]

    @pl.when(jnp.logical_not(live))
    def _():
        o_ref[...] = jnp.zeros_like(o_ref)

    @pl.when(jnp.logical_and(live, fr_ref[b] == 1))
    def _():
        w_bf[...] = w_ref[0, 0].astype(BF16)

    def half(r0):
        rows = pl.ds(r0, MOE_HALF)
        y = jnp.dot(a_ref[rows, :], w_bf[...], preferred_element_type=F32) + b_ref[0, 0]
        o_ref[rows, :] = _pack_row_bf16(y)

    @pl.when(live)
    def _():
        half(0)

    @pl.when(jnp.logical_and(live, nv_ref[b] > MOE_HALF))
    def _():
        half(MOE_HALF)

    @pl.when(jnp.logical_and(live, nv_ref[b] <= MOE_HALF))
    def _():
        o_ref[pl.ds(MOE_HALF, MOE_HALF), :] = jnp.zeros((MOE_HALF, o_ref.shape[1]), o_ref.dtype)


def _moe_down(blk_expert, n_used, blk_first, blk_rows, act, w_down, b_down, layer):
    grid_spec = pltpu.PrefetchScalarGridSpec(
        num_scalar_prefetch=4,
        grid=(MOE_BLOCKS,),
        in_specs=[
            pl.BlockSpec((MOE_BM, D_EXPERT), lambda b, be, nb, fr, nv: (_used(b, nb), 0)),
            pl.BlockSpec((1, 1, D_EXPERT, D_MODEL), lambda b, be, nb, fr, nv: (layer, be[_used(b, nb)], 0, 0)),
            pl.BlockSpec((1, 1, 1, D_MODEL), lambda b, be, nb, fr, nv: (layer, be[_used(b, nb)], 0, 0)),
        ],
        out_specs=pl.BlockSpec((MOE_BM, HALF_D), lambda b, be, nb, fr, nv: (b, 0)),
        scratch_shapes=[pltpu.VMEM((D_EXPERT, D_MODEL), BF16)],
    )
    return pl.pallas_call(
        _moe_down_kernel,
        grid_spec=grid_spec,
        out_shape=jax.ShapeDtypeStruct((MOE_ROWS, HALF_D), jnp.uint32),
        compiler_params=_cparams(("arbitrary",)),
        name="moe_down",
    )(blk_expert, n_used, blk_first, blk_rows, act, w_down, b_down.reshape(DEPTH, N_EXPERTS, 1, D_MODEL))


ROUTE_TM = 512


def _route_kernel(lg_ref, ei_ref, ew_ref, cnt_ref, base):
    i = pl.program_id(0)

    @pl.when(i == 0)
    def _():
        base[...] = jnp.zeros_like(base)

    tm = ROUTE_TM
    lg = lg_ref[...]
    lane = lax.broadcasted_iota(jnp.int32, (tm, LANE), 1)
    vals, idxs = [], []
    for _ in range(TOP_K):
        m = jnp.max(lg, axis=-1, keepdims=True)
        idx = jnp.min(jnp.where(lg == m, lane, LANE), axis=-1, keepdims=True)
        vals.append(m)
        idxs.append(idx)
        lg = jnp.where(lane == idx, -jnp.inf, lg)
    es = [jnp.exp(v - vals[0]) for v in vals]
    inv = 1.0 / functools.reduce(lambda a, b: a + b, es)
    onehot = functools.reduce(lambda a, b: a + b, [jnp.where(lane == idx, 1.0, 0.0) for idx in idxs])
    ri = lax.broadcasted_iota(jnp.int32, (tm, tm), 0)
    ci = lax.broadcasted_iota(jnp.int32, (tm, tm), 1)
    strict_lower = jnp.where(ri > ci, 1.0, 0.0).astype(BF16)
    before = jnp.dot(strict_lower, onehot.astype(BF16), preferred_element_type=F32) + base[...]
    ei = jnp.zeros((tm, LANE), jnp.int32)
    ew = jnp.zeros((tm, LANE), F32)
    for k in range(TOP_K):
        rank = jnp.sum(jnp.where(lane == idxs[k], before, 0.0), axis=-1, keepdims=True)
        ei = jnp.where(lane == k, idxs[k], ei)
        ei = jnp.where(lane == TOP_K + k, rank.astype(jnp.int32), ei)
        ew = jnp.where(lane == k, es[k] * inv, ew)
    ei_ref[...] = ei
    ew_ref[...] = ew
    base[...] = base[...] + jnp.sum(onehot, axis=0, keepdims=True)
    cnt_ref[...] = base[...]


def _route(logits):
    tm = ROUTE_TM
    return pl.pallas_call(
        _route_kernel,
        grid=(N_TOK // tm,),
        in_specs=[pl.BlockSpec((tm, LANE), lambda i: (i, 0))],
        out_specs=[
            pl.BlockSpec((tm, LANE), lambda i: (i, 0)),
            pl.BlockSpec((tm, LANE), lambda i: (i, 0)),
            pl.BlockSpec((1, LANE), lambda i: (0, 0)),
        ],
        out_shape=[
            jax.ShapeDtypeStruct((N_TOK, LANE), jnp.int32),
            jax.ShapeDtypeStruct((N_TOK, LANE), F32),
            jax.ShapeDtypeStruct((1, LANE), F32),
        ],
        scratch_shapes=[pltpu.VMEM((1, LANE), F32)],
        compiler_params=_cparams(("arbitrary",)),
        name="route",
    )(logits)


def _route_layout(ei, counts):
    cnt = counts[0, :N_EXPERTS].astype(jnp.int32)
    padded = (cnt + MOE_BM - 1) // MOE_BM * MOE_BM
    pad_end = jnp.cumsum(padded)
    pad_start = pad_end - padded
    dest = pad_start[ei[:, :TOP_K]] + ei[:, TOP_K:2 * TOP_K]
    blk_row0 = jnp.arange(MOE_BLOCKS, dtype=jnp.int32) * MOE_BM
    blk_expert = jnp.minimum(jnp.sum((pad_end[None, :] <= blk_row0[:, None]).astype(jnp.int32), axis=1),
                             N_EXPERTS - 1)
    n_used = (pad_end[-1:] // MOE_BM).astype(jnp.int32)
    blk_first = jnp.concatenate([jnp.ones((1,), jnp.int32),
                                 (blk_expert[1:] != blk_expert[:-1]).astype(jnp.int32)])
    blk_rows = jnp.clip(pad_start[blk_expert] + cnt[blk_expert] - blk_row0, 0, MOE_BM).astype(jnp.int32)
    fill_start = (pad_start + cnt).astype(jnp.int32)
    fill_cnt = (padded - cnt).astype(jnp.int32)
    dest3 = dest.astype(jnp.int32).reshape(N_TOK // MOE_TT, 1, MOE_TT * TOP_K)
    return dest3, blk_expert, n_used, blk_first, blk_rows, fill_start, fill_cnt


MOE_TT = 256


def _row_copy(src, dst, sem):
    return pltpu.make_async_copy(src, dst, sem)


def _scatter_kernel(fs_ref, fc_ref, nb_ref, dest_ref, h_ref, x_hbm, zblk, sem, zsem):
    i = pl.program_id(0)

    @pl.when(i == 0)
    def _():
        zblk[...] = jnp.zeros_like(zblk)
        zrow = zblk.at[pl.ds(0, 1)]

        def fill_expert(e, carry):
            def fill_row(r, c):
                _row_copy(zrow, x_hbm.at[pl.ds(fs_ref[e] + r, 1)], zsem).start()
                return c
            lax.fori_loop(0, fc_ref[e], fill_row, 0)

            def wait_row(r, c):
                _row_copy(zrow, x_hbm.at[pl.ds(0, 1)], zsem).wait()
                return c
            lax.fori_loop(0, fc_ref[e], wait_row, 0)
            return carry
        lax.fori_loop(0, N_EXPERTS, fill_expert, 0)

        def fill_block(b, carry):
            cp = _row_copy(zblk, x_hbm.at[pl.ds(pl.multiple_of(b * MOE_BM, MOE_BM), MOE_BM)], zsem)
            cp.start()
            cp.wait()
            return carry
        lax.fori_loop(nb_ref[0], MOE_BLOCKS, fill_block, 0)

    def issue(t, carry):
        for k in range(TOP_K):
            d = dest_ref[0, 0, t * TOP_K + k]
            _row_copy(h_ref.at[pl.ds(t, 1)], x_hbm.at[pl.ds(d, 1)], sem).start()
        return carry
    lax.fori_loop(0, MOE_TT, issue, 0, unroll=8)

    def drain(t, carry):
        _row_copy(h_ref.at[pl.ds(0, 1)], x_hbm.at[pl.ds(0, 1)], sem).wait()
        return carry
    lax.fori_loop(0, MOE_TT * TOP_K, drain, 0, unroll=8)


def _scatter_rows(fill_start, fill_cnt, n_used, dest3, h2):
    grid_spec = pltpu.PrefetchScalarGridSpec(
        num_scalar_prefetch=3,
        grid=(N_TOK // MOE_TT,),
        in_specs=[
            pl.BlockSpec((1, 1, MOE_TT * TOP_K), lambda i, fs, fc, nb: (i, 0, 0), memory_space=pltpu.SMEM),
            pl.BlockSpec((MOE_TT, HALF_D), lambda i, fs, fc, nb: (i, 0)),
        ],
        out_specs=pl.BlockSpec(memory_space=pl.ANY),
        scratch_shapes=[
            pltpu.VMEM((MOE_BM, HALF_D), jnp.uint32),
            pltpu.SemaphoreType.DMA,
            pltpu.SemaphoreType.DMA,
        ],
    )
    return pl.pallas_call(
        _scatter_kernel,
        grid_spec=grid_spec,
        out_shape=jax.ShapeDtypeStruct((MOE_ROWS, HALF_D), jnp.uint32),
        compiler_params=_cparams(("arbitrary",)),
        name="moe_scatter",
    )(fill_start, fill_cnt, n_used, dest3, h2)


def _combine_kernel(dcur_ref, dnext_ref, w_ref, x1_ref, g2_ref, y_hbm, o_ref, buf, sem):
    i = pl.program_id(0)
    n = pl.num_programs(0)
    slot = lax.rem(i, 2)

    def issue(d_ref, s):
        def body(t, carry):
            for k in range(TOP_K):
                d = d_ref[0, 0, t * TOP_K + k]
                _row_copy(y_hbm.at[pl.ds(d, 1)], buf.at[s, k, pl.ds(t, 1)], sem.at[s]).start()
            return carry
        lax.fori_loop(0, MOE_TT, body, 0, unroll=8)

    @pl.when(i == 0)
    def _():
        issue(dcur_ref, 0)

    @pl.when(i + 1 < n)
    def _():
        issue(dnext_ref, 1 - slot)

    def drain(t, carry):
        _row_copy(y_hbm.at[pl.ds(0, 1)], buf.at[slot, 0, pl.ds(0, 1)], sem.at[slot]).wait()
        return carry
    lax.fori_loop(0, MOE_TT * TOP_K, drain, 0, unroll=8)

    acc_a = acc_b = None
    for k in range(TOP_K):
        ya, yb = _unpack_row_bf16(buf[slot, k])
        wk = w_ref[:, k:k + 1]
        acc_a = ya * wk if acc_a is None else acc_a + ya * wk
        acc_b = yb * wk if acc_b is None else acc_b + yb * wk
    o_ref[:, :HALF_D] = x1_ref[:, :HALF_D] + g2_ref[0, :, :HALF_D] * acc_a
    o_ref[:, HALF_D:] = x1_ref[:, HALF_D:] + g2_ref[0, :, HALF_D:] * acc_b


def _combine(dest3, ew, x1, g2, y_rows):
    nt = N_TOK // MOE_TT
    return pl.pallas_call(
        _combine_kernel,
        grid=(nt,),
        in_specs=[
            pl.BlockSpec((1, 1, MOE_TT * TOP_K), lambda i: (i, 0, 0), memory_space=pltpu.SMEM),
            pl.BlockSpec((1, 1, MOE_TT * TOP_K), lambda i: (jnp.minimum(i + 1, nt - 1), 0, 0),
                         memory_space=pltpu.SMEM),
            pl.BlockSpec((MOE_TT, LANE), lambda i: (i, 0)),
            pl.BlockSpec((MOE_TT, D_MODEL), lambda i: (i, 0)),
            pl.BlockSpec((1, 1, D_MODEL), lambda i: (_cond_of_tile(i, MOE_TT), 0, 0)),
            pl.BlockSpec(memory_space=pl.ANY),
        ],
        out_specs=pl.BlockSpec((MOE_TT, D_MODEL), lambda i: (i, 0)),
        out_shape=jax.ShapeDtypeStruct((N_TOK, D_MODEL), F32),
        scratch_shapes=[
            pltpu.VMEM((2, TOP_K, MOE_TT, HALF_D), jnp.uint32),
            pltpu.SemaphoreType.DMA((2,)),
        ],
        compiler_params=_cparams(("arbitrary",)),
        name="moe_combine",
    )(dest3, dest3, ew, x1, g2.reshape(N_COND, 1, D_MODEL), y_rows)


def _rope_tables(n):
    t = jnp.arange(n)
    row = (t // GRID_W).astype(F32)
    col = (t % GRID_W).astype(F32)
    quarter = DA_HEADDIM // 4
    inv_freq = ROPE_BASE ** (-jnp.arange(quarter, dtype=F32) / quarter)
    ang_r = row[:, None] * inv_freq
    ang_c = col[:, None] * inv_freq
    ang = jnp.concatenate([ang_r, ang_r, ang_c, ang_c] * 2, axis=-1)
    cos, sin = jnp.cos(ang), jnp.sin(ang)
    lower = (jnp.arange(HEAD_W) % (2 * quarter)) < quarter
    return cos, jnp.where(lower, -sin, 0.0), jnp.where(lower, 0.0, sin)


def _group_mean_matrix():
    g = jnp.arange(HEAD_W) // DA_HEADDIM
    return jnp.where(g[:, None] == g[None, :], 1.0 / DA_HEADDIM, 0.0).astype(BF16)


def _layer(l, x, mod, p, cache_k, cache_v, h0_t):
    sh1, sc1, g1, sh2, sc2, g2 = [mod[l, :N_COND, i * D_MODEL:(i + 1) * D_MODEL] for i in range(6)]
    w_in = p['w_in'][l]
    o_dt = 2 * BRANCH_WIDTH + 2 * SSD_BC
    o_q = o_dt + 2 * SSD_HEADS
    o_u = o_q + 3 * BRANCH_WIDTH
    o_g = o_u + 2 * BRANCH_WIDTH
    wdt = jnp.pad(w_in[:, o_dt:o_q], ((0, 0), (0, LANE - 2 * SSD_HEADS))).astype(BF16)
    h, dt_raw = _prologue(x, p['norm1_g'][l], sc1, sh1, wdt)

    o_x = BRANCH_WIDTH
    z = _matmul(h, w_in[:, :o_x].astype(BF16), 1024, 1024, BF16, "inproj_z")
    xbc = _matmul(h, w_in[:, o_x:o_dt].astype(BF16), 1024, 1280, BF16, "inproj_xbc")
    qkv = _matmul(h, w_in[:, o_q:o_u].astype(BF16), 1024, 1024, BF16, "inproj_attn")
    uv = _matmul(h, w_in[:, o_u:o_g].astype(BF16), 1024, 1024, BF16, "inproj_sgu")
    gates = _matmul(h, w_in[:, o_g:].astype(BF16), 1024, 1024, BF16, "inproj_gates")

    xc = _conv_silu(xbc, p['conv_w'][l], p['conv_b'][l])
    dt_bias = jnp.pad(p['dt_bias'][l].reshape(1, 2 * SSD_HEADS), ((0, 0), (0, LANE - 2 * SSD_HEADS)))
    a_neg = jnp.pad(-jnp.exp(p['a_log'][l].reshape(1, 2 * SSD_HEADS)), ((0, 0), (0, LANE - 2 * SSD_HEADS)))
    y_f, y_b, h_fin = _ssd(xc, dt_raw, dt_bias, a_neg, h0_t[:, l])
    y_a = _ssd_post(y_f, y_b, xc, z, jnp.repeat(p['d_skip'][l], SSD_HEADDIM), p['ssm_norm_g'][l])
    state_l = jnp.swapaxes(h_fin[:BATCH], -1, -2)

    gq = jnp.tile(p['qk_norm_g'][l][0], 2).reshape(1, HEAD_W) * (DA_HEADDIM ** -0.5 * math.log2(math.e))
    gk = jnp.tile(p['qk_norm_g'][l][1], 2).reshape(1, HEAD_W)
    gm = _group_mean_matrix()
    lam_init = 0.8 - 0.6 * math.exp(-0.3 * l)
    lv = p['lam'][l]
    lam = (jnp.exp(jnp.sum(lv[0] * lv[1])) - jnp.exp(jnp.sum(lv[2] * lv[3])) + lam_init).reshape(1)
    qp, kp, k_store, v_store = _attn_pre(qkv, gq, gk, gm, None, row_off=0, nrows=N_PROMPT, store=True,
                                         name="attn_pre_prompt")
    kv_p = [pl.BlockSpec((SEQ, HEAD_W), lambda b, h, i: (b, h)),
            pl.BlockSpec((SEQ, HEAD_W), lambda b, h, i: (b, QKV_V_BLOCK + h))]
    o_p = _attention(lam, qp, [kp], [qkv], kv_p, p['subln_g'][l], nseq=BATCH, lq=SEQ,
                     lam_init=lam_init, name="attn_prompt")
    qs, ks = _attn_pre(qkv, gq, gk, gm, _rope_tables(DEC_SEQ), row_off=N_PROMPT, nrows=N_SAMPLE, store=False,
                       name="attn_pre_sample")
    seq0 = N_PROMPT // DEC_SEQ
    ctx_spec = pl.BlockSpec((None, None, PAST_LEN, HEAD_W), lambda b, h, i: (b, l, 0, h))
    kv_s = [pl.BlockSpec((DEC_SEQ, HEAD_W), lambda b, h, i: (b, h)), ctx_spec,
            pl.BlockSpec((DEC_SEQ, HEAD_W), lambda b, h, i: (seq0 + b, QKV_V_BLOCK + h)), ctx_spec]
    ck = cache_k.reshape(DEC_BATCH, DEPTH, PAST_LEN, BRANCH_WIDTH)
    cv = cache_v.reshape(DEC_BATCH, DEPTH, PAST_LEN, BRANCH_WIDTH)
    o_s = _attention(lam, qs, [ks, ck], [qkv, cv], kv_s, p['subln_g'][l], nseq=DEC_BATCH, lq=DEC_SEQ,
                     lam_init=lam_init, name="attn_sample")
    y_b2 = jnp.concatenate([o_p, o_s], axis=0)
    k_store = k_store.reshape(BATCH, SEQ, DA_HEADS, HEAD_W)
    v_store = v_store.reshape(BATCH, SEQ, DA_HEADS, HEAD_W)

    y_c = _sgu(uv, uv, p['sgu_norm_g'][l], p['w_s'][l], p['b_s'][l].T)

    merged = _merge(y_a, y_b2, y_c, p['w_branch'][l].astype(BF16), gates)
    wr = jnp.pad(p['w_router'][l], ((0, 0), (0, LANE - N_EXPERTS)))
    wr_hi = wr.astype(BF16)
    wr_lo = (wr - wr_hi.astype(F32)).astype(BF16)
    br = jnp.pad(p['b_router'][l].reshape(1, N_EXPERTS), ((0, 0), (0, LANE - N_EXPERTS)),
                 constant_values=-1e30)
    x1, h2, logits = _outproj(merged, p['w_out'][l].astype(BF16), x, g1, p['norm2_g'][l], sc2, sh2,
                              wr_hi, wr_lo, br)

    ei, ew, counts = _route(logits)
    dest3, blk_expert, n_used, blk_first, blk_rows, fill_start, fill_cnt = _route_layout(ei, counts)
    x_rows = _scatter_rows(fill_start, fill_cnt, n_used, dest3, h2)
    act = _moe_gu(blk_expert, n_used, blk_first, blk_rows, x_rows, p['w_gu'], p['b_gu'], l)
    y_rows = _moe_down(blk_expert, n_used, blk_first, blk_rows, act, p['w_down'], p['b_down'], l)
    x2 = _combine(dest3, ew, x1, g2, y_rows)
    return x2, k_store, v_store, state_l


def kernel(x_prompt, x_sample, c, cache_k, cache_v, state_ssm, c_ctx, w_ada, b_ada, norm1_g, norm2_g,
           w_in, conv_w, conv_b, dt_bias, a_log, d_skip, ssm_norm_g, qk_norm_g, lam, subln_g,
           sgu_norm_g, w_s, b_s, w_branch, w_out, w_router, b_router, w_gu, b_gu, w_down, b_down):
    p = {'norm1_g': norm1_g, 'norm2_g': norm2_g, 'w_in': w_in, 'conv_w': conv_w, 'conv_b': conv_b,
         'dt_bias': dt_bias, 'a_log': a_log, 'd_skip': d_skip, 'ssm_norm_g': ssm_norm_g,
         'qk_norm_g': qk_norm_g, 'lam': lam, 'subln_g': subln_g, 'sgu_norm_g': sgu_norm_g, 'w_s': w_s,
         'b_s': b_s, 'w_branch': w_branch, 'w_out': w_out, 'w_router': w_router, 'b_router': b_router,
         'w_gu': w_gu, 'b_gu': b_gu, 'w_down': w_down, 'b_down': b_down}
    cond8 = jnp.concatenate([c_ctx[None, :], c, jnp.zeros((SUBLANE - N_COND, D_MODEL), F32)], axis=0)
    mod = _ada_all(cond8, w_ada, b_ada)
    h0_t = jnp.concatenate([jnp.zeros((1,) + state_ssm.shape[1:], F32), state_ssm], axis=0)
    h0_t = jnp.swapaxes(h0_t, -1, -2)
    x = jnp.concatenate([x_prompt.reshape(N_PROMPT, D_MODEL), x_sample.reshape(N_SAMPLE, D_MODEL)], axis=0)
    ks, vs, ss = [], [], []
    for l in range(DEPTH):
        x, k_l, v_l, s_l = _layer(l, x, mod, p, cache_k, cache_v, h0_t)
        ks.append(k_l)
        vs.append(v_l)
        ss.append(s_l)
    y_prompt = x[:N_PROMPT].reshape(BATCH, SEQ, D_MODEL)
    y_sample = x[N_PROMPT:].reshape(DEC_BATCH, DEC_SEQ, D_MODEL)
    return (y_prompt, y_sample, jnp.stack(ks, axis=1), jnp.stack(vs, axis=1), jnp.stack(ss, axis=1))
```

```python
import functools
import math

import jax
import jax.numpy as jnp
import numpy as np
from jax import lax
from jax.experimental import pallas as pl
from jax.experimental.pallas import tpu as pltpu

F32 = jnp.float32
BF16 = jnp.bfloat16

D_MODEL = 2048
BATCH = 32
SEQ = 256
DEPTH = 2
DEC_BATCH = 4
DEC_SEQ = 2048
PAST_LEN = 512
GRID_W = 64
BRANCH_WIDTH = 1024
SSD_HEADDIM = 64
SSD_HEADS = 16
SSD_GROUPS = 2
SSD_STATE = 64
SSD_CONV = 7
SSD_CHUNK = 128
SSD_BC = 128
DA_HEADS = 8
DA_HEADDIM = 64
ROPE_BASE = 10000.0
SG_CHUNK = 128
SG_GROUPS = 8
N_EXPERTS = 32
TOP_K = 4
D_EXPERT = 2048
SWIGLU_LIMIT = 7.0
SWIGLU_ALPHA = 1.702
NORM_EPS = 1e-6

N_PROMPT = BATCH * SEQ
N_SAMPLE = DEC_BATCH * DEC_SEQ
N_TOK = N_PROMPT + N_SAMPLE
N_COND = 1 + DEC_BATCH

LANE = 128
SUBLANE = 8
VMEM_LIMIT = 56 * 1024 * 1024

MOE_BM = 512
MOE_BLOCKS = N_TOK * TOP_K // MOE_BM + N_EXPERTS
MOE_ROWS = MOE_BLOCKS * MOE_BM


def _cparams(sem):
    return pltpu.CompilerParams(dimension_semantics=sem, vmem_limit_bytes=VMEM_LIMIT)


def _cond_of_tile(i, tm):
    row = i * tm
    return jnp.where(row < N_PROMPT, 0, 1 + (row - N_PROMPT) // DEC_SEQ)


def _ada_kernel(c_ref, w_ref, b_ref, o_ref):
    c = c_ref[...]
    s = (c * jax.nn.sigmoid(c)).astype(BF16)
    o_ref[0] = jnp.dot(s, w_ref[0].astype(BF16), preferred_element_type=F32) + b_ref[0]


def _ada_all(cond8, w_ada, b_ada):
    tn = 1024
    n = 6 * D_MODEL
    return pl.pallas_call(
        _ada_kernel,
        grid=(DEPTH, n // tn),
        in_specs=[
            pl.BlockSpec((SUBLANE, D_MODEL), lambda l, j: (0, 0)),
            pl.BlockSpec((1, D_MODEL, tn), lambda l, j: (l, 0, j)),
            pl.BlockSpec((1, 1, tn), lambda l, j: (l, 0, j)),
        ],
        out_specs=pl.BlockSpec((1, SUBLANE, tn), lambda l, j: (l, 0, j)),
        out_shape=jax.ShapeDtypeStruct((DEPTH, SUBLANE, n), F32),
        compiler_params=_cparams(("arbitrary", "arbitrary")),
        name="ada",
    )(cond8, w_ada, b_ada.reshape(DEPTH, 1, n))


def _prologue_kernel(x_ref, g_ref, sc_ref, sh_ref, wdt_ref, h_ref, dt_ref):
    x = x_ref[...]
    ms = jnp.mean(x * x, axis=-1, keepdims=True)
    y = x * lax.rsqrt(ms + NORM_EPS) * g_ref[...]
    h = (y * (1.0 + sc_ref[0]) + sh_ref[0]).astype(BF16)
    h_ref[...] = h
    dt_ref[...] = jnp.dot(h, wdt_ref[...], preferred_element_type=F32)


def _prologue(x, g, sc, sh, wdt):
    tm = 512
    return pl.pallas_call(
        _prologue_kernel,
        grid=(N_TOK // tm,),
        in_specs=[
            pl.BlockSpec((tm, D_MODEL), lambda i: (i, 0)),
            pl.BlockSpec((1, D_MODEL), lambda i: (0, 0)),
            pl.BlockSpec((1, 1, D_MODEL), lambda i: (_cond_of_tile(i, tm), 0, 0)),
            pl.BlockSpec((1, 1, D_MODEL), lambda i: (_cond_of_tile(i, tm), 0, 0)),
            pl.BlockSpec((D_MODEL, LANE), lambda i: (0, 0)),
        ],
        out_specs=[
            pl.BlockSpec((tm, D_MODEL), lambda i: (i, 0)),
            pl.BlockSpec((tm, LANE), lambda i: (i, 0)),
        ],
        out_shape=[
            jax.ShapeDtypeStruct((N_TOK, D_MODEL), BF16),
            jax.ShapeDtypeStruct((N_TOK, LANE), F32),
        ],
        compiler_params=_cparams(("arbitrary",)),
        name="prologue",
    )(x, g.reshape(1, D_MODEL), sc.reshape(N_COND, 1, D_MODEL), sh.reshape(N_COND, 1, D_MODEL), wdt)


def _mm_kernel(a_ref, w_ref, o_ref):
    o_ref[...] = jnp.dot(a_ref[...], w_ref[...], preferred_element_type=F32).astype(o_ref.dtype)


def _matmul(a, w, tm, tn, out_dtype, name):
    m, k = a.shape
    n = w.shape[1]
    return pl.pallas_call(
        _mm_kernel,
        grid=(m // tm, n // tn),
        in_specs=[
            pl.BlockSpec((tm, k), lambda i, j: (i, 0)),
            pl.BlockSpec((k, tn), lambda i, j: (0, j)),
        ],
        out_specs=pl.BlockSpec((tm, tn), lambda i, j: (i, j)),
        out_shape=jax.ShapeDtypeStruct((m, n), out_dtype),
        compiler_params=_cparams(("arbitrary", "arbitrary")),
        name=name,
    )(a, w)


CONV_TM = 256
CONV_HALO = 16


def _conv_kernel(x_ref, prev_ref, next_ref, w_ref, b_ref, o_ref, pad_ref):
    i = pl.program_id(0)
    n_prompt_tiles = N_PROMPT // CONV_TM
    tiles_per_seq = DEC_SEQ // CONV_TM
    j = lax.rem(jnp.maximum(i - n_prompt_tiles, 0), tiles_per_seq)
    is_prompt = i < n_prompt_tiles
    first = jnp.logical_or(is_prompt, j == 0)
    last = jnp.logical_or(is_prompt, j == tiles_per_seq - 1)
    pad_ref[0:CONV_HALO, :] = jnp.where(first, 0.0, prev_ref[...].astype(F32))
    pad_ref[CONV_HALO:CONV_HALO + CONV_TM, :] = x_ref[...].astype(F32)
    pad_ref[CONV_HALO + CONV_TM:, :] = jnp.where(last, 0.0, next_ref[...].astype(F32))
    acc = jnp.broadcast_to(b_ref[...], (CONV_TM, b_ref.shape[1]))
    half = (SSD_CONV - 1) // 2
    for k in range(SSD_CONV):
        acc = acc + pad_ref[pl.ds(CONV_HALO - half + k, CONV_TM), :] * w_ref[k:k + 1, :]
    o_ref[...] = (acc * jax.nn.sigmoid(acc)).astype(o_ref.dtype)


def _conv_silu(xbc, w, b):
    wx = xbc.shape[1]
    hb = CONV_TM // CONV_HALO
    last_halo = N_TOK // CONV_HALO - 1
    return pl.pallas_call(
        _conv_kernel,
        grid=(N_TOK // CONV_TM,),
        in_specs=[
            pl.BlockSpec((CONV_TM, wx), lambda i: (i, 0)),
            pl.BlockSpec((CONV_HALO, wx), lambda i: (jnp.maximum(i * hb - 1, 0), 0)),
            pl.BlockSpec((CONV_HALO, wx), lambda i: (jnp.minimum((i + 1) * hb, last_halo), 0)),
            pl.BlockSpec((SSD_CONV, wx), lambda i: (0, 0)),
            pl.BlockSpec((1, wx), lambda i: (0, 0)),
        ],
        out_specs=pl.BlockSpec((CONV_TM, wx), lambda i: (i, 0)),
        out_shape=jax.ShapeDtypeStruct((N_TOK, wx), BF16),
        scratch_shapes=[pltpu.VMEM((CONV_TM + 2 * CONV_HALO, wx), F32)],
        compiler_params=_cparams(("arbitrary",)),
        name="ssd_conv",
    )(xbc, xbc, xbc, w, b.reshape(1, wx))


def _ssd_post_kernel(yf_ref, yb_ref, xs_ref, z_ref, d_ref, g_ref, o_ref):
    z = z_ref[...].astype(F32)
    y = yf_ref[...] + yb_ref[...] + d_ref[...] * xs_ref[...].astype(F32)
    y = y * (z * jax.nn.sigmoid(z))
    y = y * lax.rsqrt(jnp.mean(y * y, axis=-1, keepdims=True) + NORM_EPS) * g_ref[...]
    o_ref[...] = y.astype(o_ref.dtype)


def _ssd_post(y_f, y_b, xc, z, d_full, g):
    tm = 512
    row = lambda i: (i, 0)
    fixed = lambda i: (0, 0)
    return pl.pallas_call(
        _ssd_post_kernel,
        grid=(N_TOK // tm,),
        in_specs=[
            pl.BlockSpec((tm, BRANCH_WIDTH), row),
            pl.BlockSpec((tm, BRANCH_WIDTH), row),
            pl.BlockSpec((tm, BRANCH_WIDTH), row),
            pl.BlockSpec((tm, BRANCH_WIDTH), row),
            pl.BlockSpec((1, BRANCH_WIDTH), fixed),
            pl.BlockSpec((1, BRANCH_WIDTH), fixed),
        ],
        out_specs=pl.BlockSpec((tm, BRANCH_WIDTH), row),
        out_shape=jax.ShapeDtypeStruct((N_TOK, BRANCH_WIDTH), BF16),
        compiler_params=_cparams(("arbitrary",)),
        name="ssd_post",
    )(y_f, y_b, xc, z, d_full.reshape(1, BRANCH_WIDTH), g.reshape(1, BRANCH_WIDTH))


def _split3(x):
    hi = x.astype(BF16)
    r = x - hi.astype(F32)
    mid = r.astype(BF16)
    lo = (r - mid.astype(F32)).astype(BF16)
    return hi, mid, lo


def _ssd_kernel(fwd_ref, bwd_ref, first_ref, seq_ref, h0i_ref,
                xf_ref, xb_ref, dtf_ref, dtb_ref, bias_ref, a_ref, h0_ref,
                yf_ref, yb_ref, hout_ref, hstate):
    s = pl.program_id(0)
    q = SSD_CHUNK

    @pl.when(first_ref[s] == SEQ_FIRST)
    def _():
        hstate[...] = h0_ref[0]

    ri = lax.broadcasted_iota(jnp.int32, (q, q), 0)
    ci = lax.broadcasted_iota(jnp.int32, (q, q), 1)

    for d in (0, 1):
        x_ref, dt_ref, y_ref = ((xf_ref, dtf_ref, yf_ref), (xb_ref, dtb_ref, yb_ref))[d]
        xbc = x_ref[...].astype(F32)
        dt = jax.nn.softplus(dt_ref[...] + bias_ref[...])
        adt = dt * a_ref[...]
        mask = (ri >= ci) if d == 0 else (ri <= ci)
        tri = jnp.where(mask, 1.0, 0.0).astype(BF16)
        hi, mid, lo = _split3(adt)
        cum = (jnp.dot(tri, hi, preferred_element_type=F32)
               + jnp.dot(tri, mid, preferred_element_type=F32)
               + jnp.dot(tri, lo, preferred_element_type=F32))
        cum_t = cum.T
        dt_t = dt.T
        tot = cum[q - 1:q, :] if d == 0 else cum[0:1, :]
        bm = xbc[:, 1024:1152]
        cm = xbc[:, 1152:1280]
        bm_t = bm.T
        ys = []
        for g in range(SSD_GROUPS):
            cg = cm[:, g * SSD_STATE:(g + 1) * SSD_STATE]
            bg = bm[:, g * SSD_STATE:(g + 1) * SSD_STATE]
            cb = lax.dot_general(cg, bg, (((1,), (1,)), ((), ())), preferred_element_type=F32)
            bg_t = bm_t[g * SSD_STATE:(g + 1) * SSD_STATE, :]
            for hh in range(SSD_HEADS // SSD_GROUPS):
                h = g * (SSD_HEADS // SSD_GROUPS) + hh
                ch = d * SSD_HEADS + h
                pcol = cum[:, ch:ch + 1]
                prow = cum_t[ch:ch + 1, :]
                dtrow = dt_t[ch:ch + 1, :]
                lm = jnp.exp(jnp.where(mask, pcol - prow, -1e30))
                scores = cb * lm * dtrow
                xs_h = xbc[:, h * SSD_HEADDIM:(h + 1) * SSD_HEADDIM]
                h_t = hstate[d, h]
                y = jnp.dot(scores, xs_h, preferred_element_type=F32)
                y = y + jnp.dot(cg, h_t, preferred_element_type=F32) * jnp.exp(pcol)
                tot_h = tot[:, ch:ch + 1]
                w_row = dtrow * jnp.exp(tot_h - prow)
                hstate[d, h] = jnp.exp(tot_h) * h_t + jnp.dot(bg_t * w_row, xs_h, preferred_element_type=F32)
                ys.append(y)
        y_ref[...] = jnp.concatenate(ys, axis=1)

    @pl.when(first_ref[s] == SEQ_LAST)
    def _():
        zpad = jnp.zeros((q - SSD_STATE, q), F32)
        for d in (0, 1):
            for h in range(0, SSD_HEADS, 2):
                pair = jnp.concatenate([hstate[d, h], hstate[d, h + 1]], axis=1)
                pair_t = jnp.concatenate([pair, zpad], axis=0).T
                hout_ref[0, d, h] = pair_t[:SSD_HEADDIM, :SSD_STATE]
                hout_ref[0, d, h + 1] = pair_t[SSD_HEADDIM:, :SSD_STATE]


SEQ_FIRST, SEQ_LAST = 1, 2


def _ssd_tables():
    fwd, bwd, first, seq, h0i = [], [], [], [], []
    cpr = SEQ // SSD_CHUNK
    for b in range(BATCH):
        for c in range(cpr):
            fwd.append(b * cpr + c)
            bwd.append(b * cpr + (cpr - 1 - c))
            first.append(SEQ_FIRST if c == 0 else (SEQ_LAST if c == cpr - 1 else 0))
            seq.append(b)
            h0i.append(0)
    base = N_PROMPT // SSD_CHUNK
    cps = DEC_SEQ // SSD_CHUNK
    for b in range(DEC_BATCH):
        for c in range(cps):
            fwd.append(base + b * cps + c)
            bwd.append(base + b * cps + (cps - 1 - c))
            first.append(SEQ_FIRST if c == 0 else (SEQ_LAST if c == cps - 1 else 0))
            seq.append(BATCH + b)
            h0i.append(1 + b)
    return [jnp.asarray(np.asarray(t, np.int32)) for t in (fwd, bwd, first, seq, h0i)]


def _ssd(xc, dt_raw, dt_bias128, a128, h0_t):
    q = SSD_CHUNK
    nsteps = N_TOK // q
    wx = BRANCH_WIDTH + 2 * SSD_BC
    st_shape = (2, SSD_HEADS, SSD_STATE, SSD_HEADDIM)
    grid_spec = pltpu.PrefetchScalarGridSpec(
        num_scalar_prefetch=5,
        grid=(nsteps,),
        in_specs=[
            pl.BlockSpec((q, wx), lambda s, f, b, fi, sq, hi: (f[s], 0)),
            pl.BlockSpec((q, wx), lambda s, f, b, fi, sq, hi: (b[s], 0)),
            pl.BlockSpec((q, LANE), lambda s, f, b, fi, sq, hi: (f[s], 0)),
            pl.BlockSpec((q, LANE), lambda s, f, b, fi, sq, hi: (b[s], 0)),
            pl.BlockSpec((1, LANE), lambda s, f, b, fi, sq, hi: (0, 0)),
            pl.BlockSpec((1, LANE), lambda s, f, b, fi, sq, hi: (0, 0)),
            pl.BlockSpec((1,) + st_shape, lambda s, f, b, fi, sq, hi: (hi[s], 0, 0, 0, 0)),
        ],
        out_specs=[
            pl.BlockSpec((q, BRANCH_WIDTH), lambda s, f, b, fi, sq, hi: (f[s], 0)),
            pl.BlockSpec((q, BRANCH_WIDTH), lambda s, f, b, fi, sq, hi: (b[s], 0)),
            pl.BlockSpec((1,) + st_shape, lambda s, f, b, fi, sq, hi: (sq[s], 0, 0, 0, 0)),
        ],
        scratch_shapes=[pltpu.VMEM(st_shape, F32)],
    )
    return pl.pallas_call(
        _ssd_kernel,
        grid_spec=grid_spec,
        out_shape=[
            jax.ShapeDtypeStruct((N_TOK, BRANCH_WIDTH), F32),
            jax.ShapeDtypeStruct((N_TOK, BRANCH_WIDTH), F32),
            jax.ShapeDtypeStruct((BATCH + DEC_BATCH,) + st_shape, F32),
        ],
        compiler_params=_cparams(("arbitrary",)),
        name="ssd",
    )(*_ssd_tables(), xc, xc, dt_raw, dt_raw, dt_bias128, a128, h0_t)


HEAD_W = 2 * DA_HEADDIM
QKV_V_BLOCK = 2 * DA_HEADS
ATTN_PRE_TM = 512


def _attn_pre_kernel(*refs, rope, store):
    q_ref, k_ref, v_ref, gq_ref, gk_ref, gm_ref = refs[:6]
    refs = refs[6:]
    if rope:
        cos_ref, sa_ref, sb_ref = refs[:3]
        refs = refs[3:]
    qo_ref, ko_ref = refs[:2]
    if store:
        ks_ref, vs_ref = refs[2:4]
        vs_ref[...] = v_ref[...].astype(F32)
    gm = gm_ref[...]
    for x_ref, g_ref, o_ref, is_k in ((q_ref, gq_ref, qo_ref, False), (k_ref, gk_ref, ko_ref, True)):
        for s in range(DA_HEADS):
            cols = slice(s * HEAD_W, (s + 1) * HEAD_W)
            xs = x_ref[:, cols].astype(F32)
            xx = xs * xs
            hi = xx.astype(BF16)
            lo = (xx - hi.astype(F32)).astype(BF16)
            ms = jnp.dot(hi, gm, preferred_element_type=F32) + jnp.dot(lo, gm, preferred_element_type=F32)
            xn = xs * lax.rsqrt(ms + NORM_EPS) * g_ref[...]
            if store and is_k:
                ks_ref[:, cols] = xn
            if rope:
                xn = (xn * cos_ref[...] + pltpu.roll(xn, HEAD_W - 16, 1) * sa_ref[...]
                      + pltpu.roll(xn, 16, 1) * sb_ref[...])
            o_ref[:, cols] = xn.astype(o_ref.dtype)


def _attn_pre(qkv, gq, gk, gm, tables, *, row_off, nrows, store, name):
    tm = ATTN_PRE_TM
    off = row_off // tm
    rope = tables is not None
    in_specs = [
        pl.BlockSpec((tm, BRANCH_WIDTH), lambda i: (i + off, 0)),
        pl.BlockSpec((tm, BRANCH_WIDTH), lambda i: (i + off, 1)),
        pl.BlockSpec((tm, BRANCH_WIDTH), lambda i: (i + off, 2)),
        pl.BlockSpec((1, HEAD_W), lambda i: (0, 0)),
        pl.BlockSpec((1, HEAD_W), lambda i: (0, 0)),
        pl.BlockSpec((HEAD_W, HEAD_W), lambda i: (0, 0)),
    ]
    args = [qkv, qkv, qkv, gq, gk, gm]
    if rope:
        tps = DEC_SEQ // tm
        in_specs += [pl.BlockSpec((tm, HEAD_W), lambda i: (lax.rem(i, tps), 0))] * 3
        args += list(tables)
    out_specs = [pl.BlockSpec((tm, BRANCH_WIDTH), lambda i: (i, 0))] * 2
    out_shape = [jax.ShapeDtypeStruct((nrows, BRANCH_WIDTH), BF16)] * 2
    if store:
        out_specs += [pl.BlockSpec((tm, BRANCH_WIDTH), lambda i: (i, 0))] * 2
        out_shape += [jax.ShapeDtypeStruct((nrows, BRANCH_WIDTH), F32)] * 2
    return pl.pallas_call(
        functools.partial(_attn_pre_kernel, rope=rope, store=store),
        grid=(nrows // tm,),
        in_specs=in_specs,
        out_specs=out_specs,
        out_shape=out_shape,
        compiler_params=_cparams(("arbitrary",)),
        name=name,
    )(*args)


def _attn_kernel(lam_ref, q_ref, *refs, lam_init, nkv):
    k_refs = refs[:nkv]
    v_refs = refs[nkv:2 * nkv]
    g_ref, o_ref = refs[2 * nkv:]
    lam = lam_ref[0]
    qb = q_ref[...]
    ks = [k_ref[...].astype(BF16) for k_ref in k_refs]
    vs = [v_ref[...].astype(BF16) for v_ref in v_refs]
    outs = []
    for m in (0, 1):
        cols = slice(m * DA_HEADDIM, (m + 1) * DA_HEADDIM)
        ss = [lax.dot_general(qb[:, cols], kb[:, cols], (((1,), (1,)), ((), ())),
                              preferred_element_type=F32) for kb in ks]
        mx = functools.reduce(jnp.maximum, [jnp.max(s, axis=-1, keepdims=True) for s in ss])
        tot = None
        acc = None
        for s, vb in zip(ss, vs):
            e = jnp.exp2(s - mx)
            r = jnp.sum(e, axis=-1, keepdims=True)
            tot = r if tot is None else tot + r
            t = jnp.dot(e.astype(BF16), vb, preferred_element_type=F32)
            acc = t if acc is None else acc + t
        outs.append(acc * (1.0 / tot))
    o = outs[0] - lam * outs[1]
    o = o * lax.rsqrt(jnp.mean(o * o, axis=-1, keepdims=True) + NORM_EPS) * g_ref[...]
    o_ref[...] = (o * (1.0 - lam_init)).astype(o_ref.dtype)


def _attention(lam, q, ks, vs, kv_specs, subln_g, *, nseq, lq, lam_init, name):
    tq = 256
    nqb = lq // tq
    nkv = len(ks)
    grid_spec = pltpu.PrefetchScalarGridSpec(
        num_scalar_prefetch=0,
        grid=(nseq, DA_HEADS, nqb),
        in_specs=[
            pl.BlockSpec(memory_space=pltpu.SMEM),
            pl.BlockSpec((tq, HEAD_W), lambda b, h, i: (b * nqb + i, h)),
            *kv_specs,
            pl.BlockSpec((1, HEAD_W), lambda b, h, i: (0, 0)),
        ],
        out_specs=pl.BlockSpec((tq, HEAD_W), lambda b, h, i: (b * nqb + i, h)),
    )
    return pl.pallas_call(
        functools.partial(_attn_kernel, lam_init=lam_init, nkv=nkv),
        grid_spec=grid_spec,
        out_shape=jax.ShapeDtypeStruct((nseq * lq, DA_HEADS * HEAD_W), BF16),
        compiler_params=_cparams(("arbitrary", "arbitrary", "arbitrary")),
        name=name,
    )(lam, q, *ks, *vs, subln_g.reshape(1, HEAD_W))


def _sgu_kernel(u_ref, v_ref, g_ref, ws_ref, bt_ref, o_ref, *, chunks):
    v = v_ref[...].astype(F32)
    mu = jnp.mean(v, axis=-1, keepdims=True)
    vc = v - mu
    var = jnp.mean(vc * vc, axis=-1, keepdims=True)
    vn = (vc * lax.rsqrt(var + NORM_EPS) * g_ref[...]).astype(BF16)
    u = u_ref[...].astype(F32)
    gw = BRANCH_WIDTH // SG_GROUPS
    for c in range(chunks):
        rows = slice(c * SG_CHUNK, (c + 1) * SG_CHUNK)
        outs = []
        for g in range(SG_GROUPS):
            cols = slice(g * gw, (g + 1) * gw)
            mixed = jnp.dot(ws_ref[g].astype(BF16), vn[rows, cols], preferred_element_type=F32)
            mixed = mixed + bt_ref[:, g:g + 1]
            outs.append(u[rows, cols] * mixed)
        o_ref[rows, :] = jnp.concatenate(outs, axis=1).astype(o_ref.dtype)


def _sgu(u, v, g, ws, bs_t):
    chunks = 4
    tm = chunks * SG_CHUNK
    return pl.pallas_call(
        functools.partial(_sgu_kernel, chunks=chunks),
        grid=(N_TOK // tm,),
        in_specs=[
            pl.BlockSpec((tm, BRANCH_WIDTH), lambda i: (i, 0)),
            pl.BlockSpec((tm, BRANCH_WIDTH), lambda i: (i, 1)),
            pl.BlockSpec((1, BRANCH_WIDTH), lambda i: (0, 0)),
            pl.BlockSpec((SG_GROUPS, SG_CHUNK, SG_CHUNK), lambda i: (0, 0, 0)),
            pl.BlockSpec((SG_CHUNK, SG_GROUPS), lambda i: (0, 0)),
        ],
        out_specs=pl.BlockSpec((tm, BRANCH_WIDTH), lambda i: (i, 0)),
        out_shape=jax.ShapeDtypeStruct((N_TOK, BRANCH_WIDTH), BF16),
        compiler_params=_cparams(("arbitrary",)),
        name="sgu",
    )(u, v, g.reshape(1, BRANCH_WIDTH), ws, bs_t)


def _merge_kernel(ya_ref, yb_ref, yc_ref, wb_ref, ga_ref, gb_ref, gc_ref, o_ref):
    acc = None
    for k, (y_ref, g_ref) in enumerate(((ya_ref, ga_ref), (yb_ref, gb_ref), (yc_ref, gc_ref))):
        t = jnp.dot(y_ref[...], wb_ref[k], preferred_element_type=F32)
        t = t * jax.nn.sigmoid(g_ref[...].astype(F32))
        acc = t if acc is None else acc + t
    o_ref[...] = acc.astype(o_ref.dtype)


def _merge(ya, yb, yc, wb, gates):
    tm, tn = 1024, 1024
    nj = D_MODEL // tn
    yspec = pl.BlockSpec((tm, BRANCH_WIDTH), lambda i, j: (i, 0))
    return pl.pallas_call(
        _merge_kernel,
        grid=(N_TOK // tm, nj),
        in_specs=[
            yspec, yspec, yspec,
            pl.BlockSpec((3, BRANCH_WIDTH, tn), lambda i, j: (0, 0, j)),
            pl.BlockSpec((tm, tn), lambda i, j: (i, j)),
            pl.BlockSpec((tm, tn), lambda i, j: (i, nj + j)),
            pl.BlockSpec((tm, tn), lambda i, j: (i, 2 * nj + j)),
        ],
        out_specs=pl.BlockSpec((tm, tn), lambda i, j: (i, j)),
        out_shape=jax.ShapeDtypeStruct((N_TOK, D_MODEL), BF16),
        compiler_params=_cparams(("arbitrary", "arbitrary")),
        name="merge",
    )(ya, yb, yc, wb, gates, gates, gates)


HALF_D = D_MODEL // 2


def _pack_halves(hi_half, lo_half):
    top = lax.bitcast_convert_type(hi_half, jnp.uint32)
    bot = lax.shift_right_logical(lax.bitcast_convert_type(lo_half, jnp.uint32), jnp.uint32(16))
    return top | bot


def _pack_row_bf16(x):
    xb = x.astype(BF16).astype(F32)
    return _pack_halves(xb[:, :HALF_D], xb[:, HALF_D:])


def _unpack_row_bf16(p):
    a = lax.bitcast_convert_type(p & jnp.uint32(0xFFFF0000), F32)
    b = lax.bitcast_convert_type(lax.shift_left(p, jnp.uint32(16)), F32)
    return a, b


def _outproj_kernel(m_ref, w_ref, x_ref, g1_ref, n2_ref, sc_ref, sh_ref, wrh_ref, wrl_ref, br_ref,
                    x1_ref, h2_ref, lg_ref):
    out = jnp.dot(m_ref[...], w_ref[...], preferred_element_type=F32)
    x1 = x_ref[...] + g1_ref[0] * out
    x1_ref[...] = x1
    ms = jnp.mean(x1 * x1, axis=-1, keepdims=True)
    y = x1 * lax.rsqrt(ms + NORM_EPS) * n2_ref[...]
    h2 = y * (1.0 + sc_ref[0]) + sh_ref[0]
    hi = h2.astype(BF16)
    lo = (h2 - hi.astype(F32)).astype(BF16)
    h2_ref[...] = _pack_row_bf16(h2)
    wh = wrh_ref[...]
    lg = (jnp.dot(hi, wh, preferred_element_type=F32) + jnp.dot(lo, wh, preferred_element_type=F32)
          + jnp.dot(hi, wrl_ref[...], preferred_element_type=F32))
    lg_ref[...] = lg + br_ref[...]


def _outproj(merged, w_out, x, g1, n2, sc2, sh2, wr_hi, wr_lo, br):
    tm = 512
    cond = lambda i: (_cond_of_tile(i, tm), 0, 0)
    row = lambda i: (i, 0)
    fixed = lambda i: (0, 0)
    return pl.pallas_call(
        _outproj_kernel,
        grid=(N_TOK // tm,),
        in_specs=[
            pl.BlockSpec((tm, D_MODEL), row),
            pl.BlockSpec((D_MODEL, D_MODEL), fixed),
            pl.BlockSpec((tm, D_MODEL), row),
            pl.BlockSpec((1, 1, D_MODEL), cond),
            pl.BlockSpec((1, D_MODEL), fixed),
            pl.BlockSpec((1, 1, D_MODEL), cond),
            pl.BlockSpec((1, 1, D_MODEL), cond),
            pl.BlockSpec((D_MODEL, LANE), fixed),
            pl.BlockSpec((D_MODEL, LANE), fixed),
            pl.BlockSpec((1, LANE), fixed),
        ],
        out_specs=[
            pl.BlockSpec((tm, D_MODEL), row),
            pl.BlockSpec((tm, HALF_D), row),
            pl.BlockSpec((tm, LANE), row),
        ],
        out_shape=[
            jax.ShapeDtypeStruct((N_TOK, D_MODEL), F32),
            jax.ShapeDtypeStruct((N_TOK, HALF_D), jnp.uint32),
            jax.ShapeDtypeStruct((N_TOK, LANE), F32),
        ],
        compiler_params=_cparams(("arbitrary",)),
        name="outproj",
    )(merged, w_out, x, g1.reshape(N_COND, 1, D_MODEL), n2.reshape(1, D_MODEL),
      sc2.reshape(N_COND, 1, D_MODEL), sh2.reshape(N_COND, 1, D_MODEL), wr_hi, wr_lo, br)


def _used(b, nb):
    return jnp.minimum(b, nb[0] - 1)


MOE_HALF = MOE_BM // 2


def _moe_gu_kernel(be_ref, nb_ref, fr_ref, nv_ref, x_ref, wg_ref, wu_ref, bg_ref, bu_ref, o_ref, wg_bf, wu_bf):
    b = pl.program_id(1)
    live = b < nb_ref[0]

    @pl.when(jnp.logical_not(live))
    def _():
        o_ref[...] = jnp.zeros_like(o_ref)

    @pl.when(jnp.logical_and(live, fr_ref[b] == 1))
    def _():
        wg_bf[...] = wg_ref[0, 0].astype(BF16)
        wu_bf[...] = wu_ref[0, 0].astype(BF16)

    def half(r0):
        rows = pl.ds(r0, MOE_HALF)
        xa, xb = _unpack_row_bf16(x_ref[rows, :])
        xa = xa.astype(BF16)
        xb = xb.astype(BF16)
        gate = (jnp.dot(xa, wg_bf[:HALF_D, :], preferred_element_type=F32)
                + jnp.dot(xb, wg_bf[HALF_D:, :], preferred_element_type=F32) + bg_ref[0, 0])
        up = (jnp.dot(xa, wu_bf[:HALF_D, :], preferred_element_type=F32)
              + jnp.dot(xb, wu_bf[HALF_D:, :], preferred_element_type=F32) + bu_ref[0, 0])
        gate = jnp.minimum(gate, SWIGLU_LIMIT)
        up = jnp.clip(up, -SWIGLU_LIMIT, SWIGLU_LIMIT)
        act = (up + 1.0) * gate * jax.nn.sigmoid(SWIGLU_ALPHA * gate)
        o_ref[rows, :] = act.astype(o_ref.dtype)

    @pl.when(live)
    def _():
        half(0)

    @pl.when(jnp.logical_and(live, nv_ref[b] > MOE_HALF))
    def _():
        half(MOE_HALF)

    @pl.when(jnp.logical_and(live, nv_ref[b] <= MOE_HALF))
    def _():
        o_ref[pl.ds(MOE_HALF, MOE_HALF), :] = jnp.zeros((MOE_HALF, o_ref.shape[1]), o_ref.dtype)


def _moe_gu(blk_expert, n_used, blk_first, blk_rows, x_rows, w_gu, b_gu, layer):
    tn = 1024
    nj = D_EXPERT // tn
    grid_spec = pltpu.PrefetchScalarGridSpec(
        num_scalar_prefetch=4,
        grid=(nj, MOE_BLOCKS),
        in_specs=[
            pl.BlockSpec((MOE_BM, HALF_D), lambda j, b, be, nb, fr, nv: (_used(b, nb), 0)),
            pl.BlockSpec((1, 1, D_MODEL, tn), lambda j, b, be, nb, fr, nv: (layer, be[_used(b, nb)], 0, j)),
            pl.BlockSpec((1, 1, D_MODEL, tn), lambda j, b, be, nb, fr, nv: (layer, be[_used(b, nb)], 0, nj + j)),
            pl.BlockSpec((1, 1, 1, tn), lambda j, b, be, nb, fr, nv: (layer, be[_used(b, nb)], 0, j)),
            pl.BlockSpec((1, 1, 1, tn), lambda j, b, be, nb, fr, nv: (layer, be[_used(b, nb)], 0, nj + j)),
        ],
        out_specs=pl.BlockSpec((MOE_BM, tn), lambda j, b, be, nb, fr, nv: (b, j)),
        scratch_shapes=[pltpu.VMEM((D_MODEL, tn), BF16), pltpu.VMEM((D_MODEL, tn), BF16)],
    )
    b4 = b_gu.reshape(DEPTH, N_EXPERTS, 1, 2 * D_EXPERT)
    return pl.pallas_call(
        _moe_gu_kernel,
        grid_spec=grid_spec,
        out_shape=jax.ShapeDtypeStruct((MOE_ROWS, D_EXPERT), BF16),
        compiler_params=_cparams(("arbitrary", "arbitrary")),
        name="moe_gu",
    )(blk_expert, n_used, blk_first, blk_rows, x_rows, w_gu, w_gu, b4, b4)


def _moe_down_kernel(be_ref, nb_ref, fr_ref, nv_ref, a_ref, w_ref, b_ref, o_ref, w_bf):
    b = pl.program_id(0)
    live = b < nb_ref[0]

    @pl.when(jnp.logical_not(live))
    def _():
        o_ref[...] = jnp.zeros_like(o_ref)

    @pl.when(jnp.logical_and(live, fr_ref[b] == 1))
    def _():
        w_bf[...] = w_ref[0, 0].astype(BF16)

    def half(r0):
        rows = pl.ds(r0, MOE_HALF)
        y = jnp.dot(a_ref[rows, :], w_bf[...], preferred_element_type=F32) + b_ref[0, 0]
        o_ref[rows, :] = _pack_row_bf16(y)

    @pl.when(live)
    def _():
        half(0)

    @pl.when(jnp.logical_and(live, nv_ref[b] > MOE_HALF))
    def _():
        half(MOE_HALF)

    @pl.when(jnp.logical_and(live, nv_ref[b] <= MOE_HALF))
    def _():
        o_ref[pl.ds(MOE_HALF, MOE_HALF), :] = jnp.zeros((MOE_HALF, o_ref.shape[1]), o_ref.dtype)


def _moe_down(blk_expert, n_used, blk_first, blk_rows, act, w_down, b_down, layer):
    grid_spec = pltpu.PrefetchScalarGridSpec(
        num_scalar_prefetch=4,
        grid=(MOE_BLOCKS,),
        in_specs=[
            pl.BlockSpec((MOE_BM, D_EXPERT), lambda b, be, nb, fr, nv: (_used(b, nb), 0)),
            pl.BlockSpec((1, 1, D_EXPERT, D_MODEL), lambda b, be, nb, fr, nv: (layer, be[_used(b, nb)], 0, 0)),
            pl.BlockSpec((1, 1, 1, D_MODEL), lambda b, be, nb, fr, nv: (layer, be[_used(b, nb)], 0, 0)),
        ],
        out_specs=pl.BlockSpec((MOE_BM, HALF_D), lambda b, be, nb, fr, nv: (b, 0)),
        scratch_shapes=[pltpu.VMEM((D_EXPERT, D_MODEL), BF16)],
    )
    return pl.pallas_call(
        _moe_down_kernel,
        grid_spec=grid_spec,
        out_shape=jax.ShapeDtypeStruct((MOE_ROWS, HALF_D), jnp.uint32),
        compiler_params=_cparams(("arbitrary",)),
        name="moe_down",
    )(blk_expert, n_used, blk_first, blk_rows, act, w_down, b_down.reshape(DEPTH, N_EXPERTS, 1, D_MODEL))


ROUTE_TM = 512


def _route_kernel(lg_ref, ei_ref, ew_ref, cnt_ref, base):
    i = pl.program_id(0)

    @pl.when(i == 0)
    def _():
        base[...] = jnp.zeros_like(base)

    tm = ROUTE_TM
    lg = lg_ref[...]
    lane = lax.broadcasted_iota(jnp.int32, (tm, LANE), 1)
    vals, idxs = [], []
    for _ in range(TOP_K):
        m = jnp.max(lg, axis=-1, keepdims=True)
        idx = jnp.min(jnp.where(lg == m, lane, LANE), axis=-1, keepdims=True)
        vals.append(m)
        idxs.append(idx)
        lg = jnp.where(lane == idx, -jnp.inf, lg)
    es = [jnp.exp(v - vals[0]) for v in vals]
    inv = 1.0 / functools.reduce(lambda a, b: a + b, es)
    onehot = functools.reduce(lambda a, b: a + b, [jnp.where(lane == idx, 1.0, 0.0) for idx in idxs])
    ri = lax.broadcasted_iota(jnp.int32, (tm, tm), 0)
    ci = lax.broadcasted_iota(jnp.int32, (tm, tm), 1)
    strict_lower = jnp.where(ri > ci, 1.0, 0.0).astype(BF16)
    before = jnp.dot(strict_lower, onehot.astype(BF16), preferred_element_type=F32) + base[...]
    ei = jnp.zeros((tm, LANE), jnp.int32)
    ew = jnp.zeros((tm, LANE), F32)
    for k in range(TOP_K):
        rank = jnp.sum(jnp.where(lane == idxs[k], before, 0.0), axis=-1, keepdims=True)
        ei = jnp.where(lane == k, idxs[k], ei)
        ei = jnp.where(lane == TOP_K + k, rank.astype(jnp.int32), ei)
        ew = jnp.where(lane == k, es[k] * inv, ew)
    ei_ref[...] = ei
    ew_ref[...] = ew
    base[...] = base[...] + jnp.sum(onehot, axis=0, keepdims=True)
    cnt_ref[...] = base[...]


def _route(logits):
    tm = ROUTE_TM
    return pl.pallas_call(
        _route_kernel,
        grid=(N_TOK // tm,),
        in_specs=[pl.BlockSpec((tm, LANE), lambda i: (i, 0))],
        out_specs=[
            pl.BlockSpec((tm, LANE), lambda i: (i, 0)),
            pl.BlockSpec((tm, LANE), lambda i: (i, 0)),
            pl.BlockSpec((1, LANE), lambda i: (0, 0)),
        ],
        out_shape=[
            jax.ShapeDtypeStruct((N_TOK, LANE), jnp.int32),
            jax.ShapeDtypeStruct((N_TOK, LANE), F32),
            jax.ShapeDtypeStruct((1, LANE), F32),
        ],
        scratch_shapes=[pltpu.VMEM((1, LANE), F32)],
        compiler_params=_cparams(("arbitrary",)),
        name="route",
    )(logits)


def _route_layout(ei, counts):
    cnt = counts[0, :N_EXPERTS].astype(jnp.int32)
    padded = (cnt + MOE_BM - 1) // MOE_BM * MOE_BM
    pad_end = jnp.cumsum(padded)
    pad_start = pad_end - padded
    dest = pad_start[ei[:, :TOP_K]] + ei[:, TOP_K:2 * TOP_K]
    blk_row0 = jnp.arange(MOE_BLOCKS, dtype=jnp.int32) * MOE_BM
    blk_expert = jnp.minimum(jnp.sum((pad_end[None, :] <= blk_row0[:, None]).astype(jnp.int32), axis=1),
                             N_EXPERTS - 1)
    n_used = (pad_end[-1:] // MOE_BM).astype(jnp.int32)
    blk_first = jnp.concatenate([jnp.ones((1,), jnp.int32),
                                 (blk_expert[1:] != blk_expert[:-1]).astype(jnp.int32)])
    blk_rows = jnp.clip(pad_start[blk_expert] + cnt[blk_expert] - blk_row0, 0, MOE_BM).astype(jnp.int32)
    fill_start = (pad_start + cnt).astype(jnp.int32)
    fill_cnt = (padded - cnt).astype(jnp.int32)
    dest3 = dest.astype(jnp.int32).reshape(N_TOK // MOE_TT, 1, MOE_TT * TOP_K)
    return dest3, blk_expert, n_used, blk_first, blk_rows, fill_start, fill_cnt


MOE_TT = 256


def _row_copy(src, dst, sem):
    return pltpu.make_async_copy(src, dst, sem)


def _scatter_kernel(fs_ref, fc_ref, nb_ref, dest_ref, h_ref, x_hbm, zblk, sem, zsem):
    i = pl.program_id(0)

    @pl.when(i == 0)
    def _():
        zblk[...] = jnp.zeros_like(zblk)
        zrow = zblk.at[pl.ds(0, 1)]

        def fill_expert(e, carry):
            def fill_row(r, c):
                _row_copy(zrow, x_hbm.at[pl.ds(fs_ref[e] + r, 1)], zsem).start()
                return c
            lax.fori_loop(0, fc_ref[e], fill_row, 0)

            def wait_row(r, c):
                _row_copy(zrow, x_hbm.at[pl.ds(0, 1)], zsem).wait()
                return c
            lax.fori_loop(0, fc_ref[e], wait_row, 0)
            return carry
        lax.fori_loop(0, N_EXPERTS, fill_expert, 0)

        def fill_block(b, carry):
            cp = _row_copy(zblk, x_hbm.at[pl.ds(pl.multiple_of(b * MOE_BM, MOE_BM), MOE_BM)], zsem)
            cp.start()
            cp.wait()
            return carry
        lax.fori_loop(nb_ref[0], MOE_BLOCKS, fill_block, 0)

    def issue(t, carry):
        for k in range(TOP_K):
            d = dest_ref[0, 0, t * TOP_K + k]
            _row_copy(h_ref.at[pl.ds(t, 1)], x_hbm.at[pl.ds(d, 1)], sem).start()
        return carry
    lax.fori_loop(0, MOE_TT, issue, 0, unroll=8)

    def drain(t, carry):
        _row_copy(h_ref.at[pl.ds(0, 1)], x_hbm.at[pl.ds(0, 1)], sem).wait()
        return carry
    lax.fori_loop(0, MOE_TT * TOP_K, drain, 0, unroll=8)


def _scatter_rows(fill_start, fill_cnt, n_used, dest3, h2):
    grid_spec = pltpu.PrefetchScalarGridSpec(
        num_scalar_prefetch=3,
        grid=(N_TOK // MOE_TT,),
        in_specs=[
            pl.BlockSpec((1, 1, MOE_TT * TOP_K), lambda i, fs, fc, nb: (i, 0, 0), memory_space=pltpu.SMEM),
            pl.BlockSpec((MOE_TT, HALF_D), lambda i, fs, fc, nb: (i, 0)),
        ],
        out_specs=pl.BlockSpec(memory_space=pl.ANY),
        scratch_shapes=[
            pltpu.VMEM((MOE_BM, HALF_D), jnp.uint32),
            pltpu.SemaphoreType.DMA,
            pltpu.SemaphoreType.DMA,
        ],
    )
    return pl.pallas_call(
        _scatter_kernel,
        grid_spec=grid_spec,
        out_shape=jax.ShapeDtypeStruct((MOE_ROWS, HALF_D), jnp.uint32),
        compiler_params=_cparams(("arbitrary",)),
        name="moe_scatter",
    )(fill_start, fill_cnt, n_used, dest3, h2)


def _combine_kernel(dcur_ref, dnext_ref, w_ref, x1_ref, g2_ref, y_hbm, o_ref, buf, sem):
    i = pl.program_id(0)
    n = pl.num_programs(0)
    slot = lax.rem(i, 2)

    def issue(d_ref, s):
        def body(t, carry):
            for k in range(TOP_K):
                d = d_ref[0, 0, t * TOP_K + k]
                _row_copy(y_hbm.at[pl.ds(d, 1)], buf.at[s, k, pl.ds(t, 1)], sem.at[s]).start()
            return carry
        lax.fori_loop(0, MOE_TT, body, 0, unroll=8)

    @pl.when(i == 0)
    def _():
        issue(dcur_ref, 0)

    @pl.when(i + 1 < n)
    def _():
        issue(dnext_ref, 1 - slot)

    def drain(t, carry):
        _row_copy(y_hbm.at[pl.ds(0, 1)], buf.at[slot, 0, pl.ds(0, 1)], sem.at[slot]).wait()
        return carry
    lax.fori_loop(0, MOE_TT * TOP_K, drain, 0, unroll=8)

    acc_a = acc_b = None
    for k in range(TOP_K):
        ya, yb = _unpack_row_bf16(buf[slot, k])
        wk = w_ref[:, k:k + 1]
        acc_a = ya * wk if acc_a is None else acc_a + ya * wk
        acc_b = yb * wk if acc_b is None else acc_b + yb * wk
    o_ref[:, :HALF_D] = x1_ref[:, :HALF_D] + g2_ref[0, :, :HALF_D] * acc_a
    o_ref[:, HALF_D:] = x1_ref[:, HALF_D:] + g2_ref[0, :, HALF_D:] * acc_b


def _combine(dest3, ew, x1, g2, y_rows):
    nt = N_TOK // MOE_TT
    return pl.pallas_call(
        _combine_kernel,
        grid=(nt,),
        in_specs=[
            pl.BlockSpec((1, 1, MOE_TT * TOP_K), lambda i: (i, 0, 0), memory_space=pltpu.SMEM),
            pl.BlockSpec((1, 1, MOE_TT * TOP_K), lambda i: (jnp.minimum(i + 1, nt - 1), 0, 0),
                         memory_space=pltpu.SMEM),
            pl.BlockSpec((MOE_TT, LANE), lambda i: (i, 0)),
            pl.BlockSpec((MOE_TT, D_MODEL), lambda i: (i, 0)),
            pl.BlockSpec((1, 1, D_MODEL), lambda i: (_cond_of_tile(i, MOE_TT), 0, 0)),
            pl.BlockSpec(memory_space=pl.ANY),
        ],
        out_specs=pl.BlockSpec((MOE_TT, D_MODEL), lambda i: (i, 0)),
        out_shape=jax.ShapeDtypeStruct((N_TOK, D_MODEL), F32),
        scratch_shapes=[
            pltpu.VMEM((2, TOP_K, MOE_TT, HALF_D), jnp.uint32),
            pltpu.SemaphoreType.DMA((2,)),
        ],
        compiler_params=_cparams(("arbitrary",)),
        name="moe_combine",
    )(dest3, dest3, ew, x1, g2.reshape(N_COND, 1, D_MODEL), y_rows)


def _rope_tables(n):
    t = jnp.arange(n)
    row = (t // GRID_W).astype(F32)
    col = (t % GRID_W).astype(F32)
    quarter = DA_HEADDIM // 4
    inv_freq = ROPE_BASE ** (-jnp.arange(quarter, dtype=F32) / quarter)
    ang_r = row[:, None] * inv_freq
    ang_c = col[:, None] * inv_freq
    ang = jnp.concatenate([ang_r, ang_r, ang_c, ang_c] * 2, axis=-1)
    cos, sin = jnp.cos(ang), jnp.sin(ang)
    lower = (jnp.arange(HEAD_W) % (2 * quarter)) < quarter
    return cos, jnp.where(lower, -sin, 0.0), jnp.where(lower, 0.0, sin)


def _group_mean_matrix():
    g = jnp.arange(HEAD_W) // DA_HEADDIM
    return jnp.where(g[:, None] == g[None, :], 1.0 / DA_HEADDIM, 0.0).astype(BF16)


def _layer(l, x, mod, p, cache_k, cache_v, h0_t):
    sh1, sc1, g1, sh2, sc2, g2 = [mod[l, :N_COND, i * D_MODEL:(i + 1) * D_MODEL] for i in range(6)]
    w_in = p['w_in'][l]
    o_dt = 2 * BRANCH_WIDTH + 2 * SSD_BC
    o_q = o_dt + 2 * SSD_HEADS
    o_u = o_q + 3 * BRANCH_WIDTH
    o_g = o_u + 2 * BRANCH_WIDTH
    wdt = jnp.pad(w_in[:, o_dt:o_q], ((0, 0), (0, LANE - 2 * SSD_HEADS))).astype(BF16)
    h, dt_raw = _prologue(x, p['norm1_g'][l], sc1, sh1, wdt)

    o_x = BRANCH_WIDTH
    z = _matmul(h, w_in[:, :o_x].astype(BF16), 1024, 1024, BF16, "inproj_z")
    xbc = _matmul(h, w_in[:, o_x:o_dt].astype(BF16), 1024, 1280, BF16, "inproj_xbc")
    qkv = _matmul(h, w_in[:, o_q:o_u].astype(BF16), 1024, 1024, BF16, "inproj_attn")
    uv = _matmul(h, w_in[:, o_u:o_g].astype(BF16), 1024, 1024, BF16, "inproj_sgu")
    gates = _matmul(h, w_in[:, o_g:].astype(BF16), 1024, 1024, BF16, "inproj_gates")

    xc = _conv_silu(xbc, p['conv_w'][l], p['conv_b'][l])
    dt_bias = jnp.pad(p['dt_bias'][l].reshape(1, 2 * SSD_HEADS), ((0, 0), (0, LANE - 2 * SSD_HEADS)))
    a_neg = jnp.pad(-jnp.exp(p['a_log'][l].reshape(1, 2 * SSD_HEADS)), ((0, 0), (0, LANE - 2 * SSD_HEADS)))
    y_f, y_b, h_fin = _ssd(xc, dt_raw, dt_bias, a_neg, h0_t[:, l])
    y_a = _ssd_post(y_f, y_b, xc, z, jnp.repeat(p['d_skip'][l], SSD_HEADDIM), p['ssm_norm_g'][l])
    state_l = h_fin[:BATCH]

    gq = jnp.tile(p['qk_norm_g'][l][0], 2).reshape(1, HEAD_W) * (DA_HEADDIM ** -0.5 * math.log2(math.e))
    gk = jnp.tile(p['qk_norm_g'][l][1], 2).reshape(1, HEAD_W)
    gm = _group_mean_matrix()
    lam_init = 0.8 - 0.6 * math.exp(-0.3 * l)
    lv = p['lam'][l]
    lam = (jnp.exp(jnp.sum(lv[0] * lv[1])) - jnp.exp(jnp.sum(lv[2] * lv[3])) + lam_init).reshape(1)
    qp, kp, k_store, v_store = _attn_pre(qkv, gq, gk, gm, None, row_off=0, nrows=N_PROMPT, store=True,
                                         name="attn_pre_prompt")
    kv_p = [pl.BlockSpec((SEQ, HEAD_W), lambda b, h, i: (b, h)),
            pl.BlockSpec((SEQ, HEAD_W), lambda b, h, i: (b, QKV_V_BLOCK + h))]
    o_p = _attention(lam, qp, [kp], [qkv], kv_p, p['subln_g'][l], nseq=BATCH, lq=SEQ,
                     lam_init=lam_init, name="attn_prompt")
    qs, ks = _attn_pre(qkv, gq, gk, gm, _rope_tables(DEC_SEQ), row_off=N_PROMPT, nrows=N_SAMPLE, store=False,
                       name="attn_pre_sample")
    seq0 = N_PROMPT // DEC_SEQ
    ctx_spec = pl.BlockSpec((None, None, PAST_LEN, HEAD_W), lambda b, h, i: (b, l, 0, h))
    kv_s = [pl.BlockSpec((DEC_SEQ, HEAD_W), lambda b, h, i: (b, h)), ctx_spec,
            pl.BlockSpec((DEC_SEQ, HEAD_W), lambda b, h, i: (seq0 + b, QKV_V_BLOCK + h)), ctx_spec]
    ck = cache_k.reshape(DEC_BATCH, DEPTH, PAST_LEN, BRANCH_WIDTH)
    cv = cache_v.reshape(DEC_BATCH, DEPTH, PAST_LEN, BRANCH_WIDTH)
    o_s = _attention(lam, qs, [ks, ck], [qkv, cv], kv_s, p['subln_g'][l], nseq=DEC_BATCH, lq=DEC_SEQ,
                     lam_init=lam_init, name="attn_sample")
    y_b2 = jnp.concatenate([o_p, o_s], axis=0)
    k_store = k_store.reshape(BATCH, SEQ, DA_HEADS, HEAD_W)
    v_store = v_store.reshape(BATCH, SEQ, DA_HEADS, HEAD_W)

    y_c = _sgu(uv, uv, p['sgu_norm_g'][l], p['w_s'][l], p['b_s'][l].T)

    merged = _merge(y_a, y_b2, y_c, p['w_branch'][l].astype(BF16), gates)
    wr = jnp.pad(p['w_router'][l], ((0, 0), (0, LANE - N_EXPERTS)))
    wr_hi = wr.astype(BF16)
    wr_lo = (wr - wr_hi.astype(F32)).astype(BF16)
    br = jnp.pad(p['b_router'][l].reshape(1, N_EXPERTS), ((0, 0), (0, LANE - N_EXPERTS)),
                 constant_values=-1e30)
    x1, h2, logits = _outproj(merged, p['w_out'][l].astype(BF16), x, g1, p['norm2_g'][l], sc2, sh2,
                              wr_hi, wr_lo, br)

    ei, ew, counts = _route(logits)
    dest3, blk_expert, n_used, blk_first, blk_rows, fill_start, fill_cnt = _route_layout(ei, counts)
    x_rows = _scatter_rows(fill_start, fill_cnt, n_used, dest3, h2)
    act = _moe_gu(blk_expert, n_used, blk_first, blk_rows, x_rows, p['w_gu'], p['b_gu'], l)
    y_rows = _moe_down(blk_expert, n_used, blk_first, blk_rows, act, p['w_down'], p['b_down'], l)
    x2 = _combine(dest3, ew, x1, g2, y_rows)
    return x2, k_store, v_store, state_l


def kernel(x_prompt, x_sample, c, cache_k, cache_v, state_ssm, c_ctx, w_ada, b_ada, norm1_g, norm2_g,
           w_in, conv_w, conv_b, dt_bias, a_log, d_skip, ssm_norm_g, qk_norm_g, lam, subln_g,
           sgu_norm_g, w_s, b_s, w_branch, w_out, w_router, b_router, w_gu, b_gu, w_down, b_down):
    p = {'norm1_g': norm1_g, 'norm2_g': norm2_g, 'w_in': w_in, 'conv_w': conv_w, 'conv_b': conv_b,
         'dt_bias': dt_bias, 'a_log': a_log, 'd_skip': d_skip, 'ssm_norm_g': ssm_norm_g,
         'qk_norm_g': qk_norm_g, 'lam': lam, 'subln_g': subln_g, 'sgu_norm_g': sgu_norm_g, 'w_s': w_s,
         'b_s': b_s, 'w_branch': w_branch, 'w_out': w_out, 'w_router': w_router, 'b_router': b_router,
         'w_gu': w_gu, 'b_gu': b_gu, 'w_down': w_down, 'b_down': b_down}
    cond8 = jnp.concatenate([c_ctx[None, :], c, jnp.zeros((SUBLANE - N_COND, D_MODEL), F32)], axis=0)
    mod = _ada_all(cond8, w_ada, b_ada)
    h0_t = jnp.concatenate([jnp.zeros((1,) + state_ssm.shape[1:], F32), state_ssm], axis=0)
    h0_t = jnp.swapaxes(h0_t, -1, -2)
    x = jnp.concatenate([x_prompt.reshape(N_PROMPT, D_MODEL), x_sample.reshape(N_SAMPLE, D_MODEL)], axis=0)
    ks, vs, ss = [], [], []
    for l in range(DEPTH):
        x, k_l, v_l, s_l = _layer(l, x, mod, p, cache_k, cache_v, h0_t)
        ks.append(k_l)
        vs.append(v_l)
        ss.append(s_l)
    y_prompt = x[:N_PROMPT].reshape(BATCH, SEQ, D_MODEL)
    y_sample = x[N_PROMPT:].reshape(DEC_BATCH, DEC_SEQ, D_MODEL)
    return (y_prompt, y_sample, jnp.stack(ks, axis=1), jnp.stack(vs, axis=1), jnp.stack(ss, axis=1))
```

```python
import functools
import math

import jax
import jax.numpy as jnp
import numpy as np
from jax import lax
from jax.experimental import pallas as pl
from jax.experimental.pallas import tpu as pltpu

F32 = jnp.float32
BF16 = jnp.bfloat16

D_MODEL = 2048
BATCH = 32
SEQ = 256
DEPTH = 2
DEC_BATCH = 4
DEC_SEQ = 2048
PAST_LEN = 512
GRID_W = 64
BRANCH_WIDTH = 1024
SSD_HEADDIM = 64
SSD_HEADS = 16
SSD_GROUPS = 2
SSD_STATE = 64
SSD_CONV = 7
SSD_CHUNK = 128
SSD_BC = 128
DA_HEADS = 8
DA_HEADDIM = 64
ROPE_BASE = 10000.0
SG_CHUNK = 128
SG_GROUPS = 8
N_EXPERTS = 32
TOP_K = 4
D_EXPERT = 2048
SWIGLU_LIMIT = 7.0
SWIGLU_ALPHA = 1.702
NORM_EPS = 1e-6

N_PROMPT = BATCH * SEQ
N_SAMPLE = DEC_BATCH * DEC_SEQ
N_TOK = N_PROMPT + N_SAMPLE
N_COND = 1 + DEC_BATCH

LANE = 128
SUBLANE = 8
VMEM_LIMIT = 56 * 1024 * 1024

MOE_BM = 512
MOE_BLOCKS = N_TOK * TOP_K // MOE_BM + N_EXPERTS
MOE_ROWS = MOE_BLOCKS * MOE_BM


def _cparams(sem):
    return pltpu.CompilerParams(dimension_semantics=sem, vmem_limit_bytes=VMEM_LIMIT)


def _cond_of_tile(i, tm):
    row = i * tm
    return jnp.where(row < N_PROMPT, 0, 1 + (row - N_PROMPT) // DEC_SEQ)


def _ada_kernel(c_ref, w_ref, b_ref, o_ref):
    c = c_ref[...]
    s = (c * jax.nn.sigmoid(c)).astype(BF16)
    o_ref[0] = jnp.dot(s, w_ref[0].astype(BF16), preferred_element_type=F32) + b_ref[0]


def _ada_all(cond8, w_ada, b_ada):
    tn = 1024
    n = 6 * D_MODEL
    return pl.pallas_call(
        _ada_kernel,
        grid=(DEPTH, n // tn),
        in_specs=[
            pl.BlockSpec((SUBLANE, D_MODEL), lambda l, j: (0, 0)),
            pl.BlockSpec((1, D_MODEL, tn), lambda l, j: (l, 0, j)),
            pl.BlockSpec((1, 1, tn), lambda l, j: (l, 0, j)),
        ],
        out_specs=pl.BlockSpec((1, SUBLANE, tn), lambda l, j: (l, 0, j)),
        out_shape=jax.ShapeDtypeStruct((DEPTH, SUBLANE, n), F32),
        compiler_params=_cparams(("arbitrary", "arbitrary")),
        name="ada",
    )(cond8, w_ada, b_ada.reshape(DEPTH, 1, n))


def _prologue_kernel(x_ref, g_ref, sc_ref, sh_ref, wdt_ref, h_ref, dt_ref):
    x = x_ref[...]
    ms = jnp.mean(x * x, axis=-1, keepdims=True)
    y = x * lax.rsqrt(ms + NORM_EPS) * g_ref[...]
    h = (y * (1.0 + sc_ref[0]) + sh_ref[0]).astype(BF16)
    h_ref[...] = h
    dt_ref[...] = jnp.dot(h, wdt_ref[...], preferred_element_type=F32)


def _prologue(x, g, sc, sh, wdt):
    tm = 512
    return pl.pallas_call(
        _prologue_kernel,
        grid=(N_TOK // tm,),
        in_specs=[
            pl.BlockSpec((tm, D_MODEL), lambda i: (i, 0)),
            pl.BlockSpec((1, D_MODEL), lambda i: (0, 0)),
            pl.BlockSpec((1, 1, D_MODEL), lambda i: (_cond_of_tile(i, tm), 0, 0)),
            pl.BlockSpec((1, 1, D_MODEL), lambda i: (_cond_of_tile(i, tm), 0, 0)),
            pl.BlockSpec((D_MODEL, LANE), lambda i: (0, 0)),
        ],
        out_specs=[
            pl.BlockSpec((tm, D_MODEL), lambda i: (i, 0)),
            pl.BlockSpec((tm, LANE), lambda i: (i, 0)),
        ],
        out_shape=[
            jax.ShapeDtypeStruct((N_TOK, D_MODEL), BF16),
            jax.ShapeDtypeStruct((N_TOK, LANE), F32),
        ],
        compiler_params=_cparams(("arbitrary",)),
        name="prologue",
    )(x, g.reshape(1, D_MODEL), sc.reshape(N_COND, 1, D_MODEL), sh.reshape(N_COND, 1, D_MODEL), wdt)


def _mm_kernel(a_ref, w_ref, o_ref):
    o_ref[...] = jnp.dot(a_ref[...], w_ref[...], preferred_element_type=F32).astype(o_ref.dtype)


def _matmul(a, w, tm, tn, out_dtype, name):
    m, k = a.shape
    n = w.shape[1]
    return pl.pallas_call(
        _mm_kernel,
        grid=(m // tm, n // tn),
        in_specs=[
            pl.BlockSpec((tm, k), lambda i, j: (i, 0)),
            pl.BlockSpec((k, tn), lambda i, j: (0, j)),
        ],
        out_specs=pl.BlockSpec((tm, tn), lambda i, j: (i, j)),
        out_shape=jax.ShapeDtypeStruct((m, n), out_dtype),
        compiler_params=_cparams(("arbitrary", "arbitrary")),
        name=name,
    )(a, w)


CONV_TM = 256
CONV_HALO = 16


def _conv_kernel(x_ref, prev_ref, next_ref, w_ref, b_ref, o_ref, pad_ref):
    i = pl.program_id(0)
    n_prompt_tiles = N_PROMPT // CONV_TM
    tiles_per_seq = DEC_SEQ // CONV_TM
    j = lax.rem(jnp.maximum(i - n_prompt_tiles, 0), tiles_per_seq)
    is_prompt = i < n_prompt_tiles
    first = jnp.logical_or(is_prompt, j == 0)
    last = jnp.logical_or(is_prompt, j == tiles_per_seq - 1)
    pad_ref[0:CONV_HALO, :] = jnp.where(first, 0.0, prev_ref[...].astype(F32))
    pad_ref[CONV_HALO:CONV_HALO + CONV_TM, :] = x_ref[...].astype(F32)
    pad_ref[CONV_HALO + CONV_TM:, :] = jnp.where(last, 0.0, next_ref[...].astype(F32))
    acc = jnp.broadcast_to(b_ref[...], (CONV_TM, b_ref.shape[1]))
    half = (SSD_CONV - 1) // 2
    for k in range(SSD_CONV):
        acc = acc + pad_ref[pl.ds(CONV_HALO - half + k, CONV_TM), :] * w_ref[k:k + 1, :]
    o_ref[...] = (acc * jax.nn.sigmoid(acc)).astype(o_ref.dtype)


def _conv_silu(xbc, w, b):
    wx = xbc.shape[1]
    hb = CONV_TM // CONV_HALO
    last_halo = N_TOK // CONV_HALO - 1
    return pl.pallas_call(
        _conv_kernel,
        grid=(N_TOK // CONV_TM,),
        in_specs=[
            pl.BlockSpec((CONV_TM, wx), lambda i: (i, 0)),
            pl.BlockSpec((CONV_HALO, wx), lambda i: (jnp.maximum(i * hb - 1, 0), 0)),
            pl.BlockSpec((CONV_HALO, wx), lambda i: (jnp.minimum((i + 1) * hb, last_halo), 0)),
            pl.BlockSpec((SSD_CONV, wx), lambda i: (0, 0)),
            pl.BlockSpec((1, wx), lambda i: (0, 0)),
        ],
        out_specs=pl.BlockSpec((CONV_TM, wx), lambda i: (i, 0)),
        out_shape=jax.ShapeDtypeStruct((N_TOK, wx), BF16),
        scratch_shapes=[pltpu.VMEM((CONV_TM + 2 * CONV_HALO, wx), F32)],
        compiler_params=_cparams(("arbitrary",)),
        name="ssd_conv",
    )(xbc, xbc, xbc, w, b.reshape(1, wx))


def _ssd_post_kernel(yf_ref, yb_ref, xs_ref, z_ref, d_ref, g_ref, o_ref):
    z = z_ref[...].astype(F32)
    y = yf_ref[...] + yb_ref[...] + d_ref[...] * xs_ref[...].astype(F32)
    y = y * (z * jax.nn.sigmoid(z))
    y = y * lax.rsqrt(jnp.mean(y * y, axis=-1, keepdims=True) + NORM_EPS) * g_ref[...]
    o_ref[...] = y.astype(o_ref.dtype)


def _ssd_post(y_f, y_b, xc, z, d_full, g):
    tm = 512
    row = lambda i: (i, 0)
    fixed = lambda i: (0, 0)
    return pl.pallas_call(
        _ssd_post_kernel,
        grid=(N_TOK // tm,),
        in_specs=[
            pl.BlockSpec((tm, BRANCH_WIDTH), row),
            pl.BlockSpec((tm, BRANCH_WIDTH), row),
            pl.BlockSpec((tm, BRANCH_WIDTH), row),
            pl.BlockSpec((tm, BRANCH_WIDTH), row),
            pl.BlockSpec((1, BRANCH_WIDTH), fixed),
            pl.BlockSpec((1, BRANCH_WIDTH), fixed),
        ],
        out_specs=pl.BlockSpec((tm, BRANCH_WIDTH), row),
        out_shape=jax.ShapeDtypeStruct((N_TOK, BRANCH_WIDTH), BF16),
        compiler_params=_cparams(("arbitrary",)),
        name="ssd_post",
    )(y_f, y_b, xc, z, d_full.reshape(1, BRANCH_WIDTH), g.reshape(1, BRANCH_WIDTH))


def _split3(x):
    hi = x.astype(BF16)
    r = x - hi.astype(F32)
    mid = r.astype(BF16)
    lo = (r - mid.astype(F32)).astype(BF16)
    return hi, mid, lo


def _ssd_kernel(fwd_ref, bwd_ref, first_ref, seq_ref, h0i_ref,
                xf_ref, xb_ref, dtf_ref, dtb_ref, bias_ref, a_ref, h0_ref,
                yf_ref, yb_ref, hout_ref, hstate):
    s = pl.program_id(0)
    q = SSD_CHUNK

    @pl.when(first_ref[s] == 1)
    def _():
        hstate[...] = h0_ref[0]

    ri = lax.broadcasted_iota(jnp.int32, (q, q), 0)
    ci = lax.broadcasted_iota(jnp.int32, (q, q), 1)

    for d in (0, 1):
        x_ref, dt_ref, y_ref = ((xf_ref, dtf_ref, yf_ref), (xb_ref, dtb_ref, yb_ref))[d]
        xbc = x_ref[...].astype(F32)
        dt = jax.nn.softplus(dt_ref[...] + bias_ref[...])
        adt = dt * a_ref[...]
        mask = (ri >= ci) if d == 0 else (ri <= ci)
        tri = jnp.where(mask, 1.0, 0.0).astype(BF16)
        hi, mid, lo = _split3(adt)
        cum = (jnp.dot(tri, hi, preferred_element_type=F32)
               + jnp.dot(tri, mid, preferred_element_type=F32)
               + jnp.dot(tri, lo, preferred_element_type=F32))
        cum_t = cum.T
        dt_t = dt.T
        tot = cum[q - 1:q, :] if d == 0 else cum[0:1, :]
        bm = xbc[:, 1024:1152]
        cm = xbc[:, 1152:1280]
        bm_t = bm.T
        ys = []
        for g in range(SSD_GROUPS):
            cg = cm[:, g * SSD_STATE:(g + 1) * SSD_STATE]
            bg = bm[:, g * SSD_STATE:(g + 1) * SSD_STATE]
            cb = lax.dot_general(cg, bg, (((1,), (1,)), ((), ())), preferred_element_type=F32)
            bg_t = bm_t[g * SSD_STATE:(g + 1) * SSD_STATE, :]
            for hh in range(SSD_HEADS // SSD_GROUPS):
                h = g * (SSD_HEADS // SSD_GROUPS) + hh
                ch = d * SSD_HEADS + h
                pcol = cum[:, ch:ch + 1]
                prow = cum_t[ch:ch + 1, :]
                dtrow = dt_t[ch:ch + 1, :]
                lm = jnp.exp(jnp.where(mask, pcol - prow, -1e30))
                scores = cb * lm * dtrow
                xs_h = xbc[:, h * SSD_HEADDIM:(h + 1) * SSD_HEADDIM]
                h_t = hstate[d, h]
                y = jnp.dot(scores, xs_h, preferred_element_type=F32)
                y = y + jnp.dot(cg, h_t, preferred_element_type=F32) * jnp.exp(pcol)
                tot_h = tot[:, ch:ch + 1]
                w_row = dtrow * jnp.exp(tot_h - prow)
                hstate[d, h] = jnp.exp(tot_h) * h_t + jnp.dot(bg_t * w_row, xs_h, preferred_element_type=F32)
                ys.append(y)
        y_ref[...] = jnp.concatenate(ys, axis=1)
    hout_ref[0] = hstate[...]


def _ssd_tables():
    fwd, bwd, first, seq, h0i = [], [], [], [], []
    cpr = SEQ // SSD_CHUNK
    for b in range(BATCH):
        for c in range(cpr):
            fwd.append(b * cpr + c)
            bwd.append(b * cpr + (cpr - 1 - c))
            first.append(1 if c == 0 else 0)
            seq.append(b)
            h0i.append(0)
    base = N_PROMPT // SSD_CHUNK
    cps = DEC_SEQ // SSD_CHUNK
    for b in range(DEC_BATCH):
        for c in range(cps):
            fwd.append(base + b * cps + c)
            bwd.append(base + b * cps + (cps - 1 - c))
            first.append(1 if c == 0 else 0)
            seq.append(BATCH + b)
            h0i.append(1 + b)
    return [jnp.asarray(np.asarray(t, np.int32)) for t in (fwd, bwd, first, seq, h0i)]


def _ssd(xc, dt_raw, dt_bias128, a128, h0_t):
    q = SSD_CHUNK
    nsteps = N_TOK // q
    wx = BRANCH_WIDTH + 2 * SSD_BC
    st_shape = (2, SSD_HEADS, SSD_STATE, SSD_HEADDIM)
    grid_spec = pltpu.PrefetchScalarGridSpec(
        num_scalar_prefetch=5,
        grid=(nsteps,),
        in_specs=[
            pl.BlockSpec((q, wx), lambda s, f, b, fi, sq, hi: (f[s], 0)),
            pl.BlockSpec((q, wx), lambda s, f, b, fi, sq, hi: (b[s], 0)),
            pl.BlockSpec((q, LANE), lambda s, f, b, fi, sq, hi: (f[s], 0)),
            pl.BlockSpec((q, LANE), lambda s, f, b, fi, sq, hi: (b[s], 0)),
            pl.BlockSpec((1, LANE), lambda s, f, b, fi, sq, hi: (0, 0)),
            pl.BlockSpec((1, LANE), lambda s, f, b, fi, sq, hi: (0, 0)),
            pl.BlockSpec((1,) + st_shape, lambda s, f, b, fi, sq, hi: (hi[s], 0, 0, 0, 0)),
        ],
        out_specs=[
            pl.BlockSpec((q, BRANCH_WIDTH), lambda s, f, b, fi, sq, hi: (f[s], 0)),
            pl.BlockSpec((q, BRANCH_WIDTH), lambda s, f, b, fi, sq, hi: (b[s], 0)),
            pl.BlockSpec((1,) + st_shape, lambda s, f, b, fi, sq, hi: (sq[s], 0, 0, 0, 0)),
        ],
        scratch_shapes=[pltpu.VMEM(st_shape, F32)],
    )
    return pl.pallas_call(
        _ssd_kernel,
        grid_spec=grid_spec,
        out_shape=[
            jax.ShapeDtypeStruct((N_TOK, BRANCH_WIDTH), F32),
            jax.ShapeDtypeStruct((N_TOK, BRANCH_WIDTH), F32),
            jax.ShapeDtypeStruct((BATCH + DEC_BATCH,) + st_shape, F32),
        ],
        compiler_params=_cparams(("arbitrary",)),
        name="ssd",
    )(*_ssd_tables(), xc, xc, dt_raw, dt_raw, dt_bias128, a128, h0_t)


HEAD_W = 2 * DA_HEADDIM
QKV_V_BLOCK = 2 * DA_HEADS
ATTN_PRE_TM = 512
ATTN_TQ = 256


def _attn_pre_kernel(*refs, rope, store):
    q_ref, k_ref, v_ref, gq_ref, gk_ref, gm_ref = refs[:6]
    refs = refs[6:]
    if rope:
        cos_ref, sa_ref, sb_ref = refs[:3]
        refs = refs[3:]
    qo_ref, ko_ref = refs[:2]
    if store:
        ks_ref, vs_ref = refs[2:4]
        vs_ref[...] = v_ref[...].astype(F32)
    gm = gm_ref[...]
    for x_ref, g_ref, o_ref, is_k in ((q_ref, gq_ref, qo_ref, False), (k_ref, gk_ref, ko_ref, True)):
        for s in range(DA_HEADS):
            cols = slice(s * HEAD_W, (s + 1) * HEAD_W)
            xs = x_ref[:, cols].astype(F32)
            xx = xs * xs
            hi = xx.astype(BF16)
            lo = (xx - hi.astype(F32)).astype(BF16)
            ms = jnp.dot(hi, gm, preferred_element_type=F32) + jnp.dot(lo, gm, preferred_element_type=F32)
            xn = xs * lax.rsqrt(ms + NORM_EPS) * g_ref[...]
            if store and is_k:
                ks_ref[:, cols] = xn
            if rope:
                xn = (xn * cos_ref[...] + pltpu.roll(xn, HEAD_W - 16, 1) * sa_ref[...]
                      + pltpu.roll(xn, 16, 1) * sb_ref[...])
            o_ref[:, cols] = xn.astype(o_ref.dtype)


def _attn_pre(qkv, gq, gk, gm, tables, *, row_off, nrows, store, name):
    tm = ATTN_PRE_TM
    off = row_off // tm
    rope = tables is not None
    in_specs = [
        pl.BlockSpec((tm, BRANCH_WIDTH), lambda i: (i + off, 0)),
        pl.BlockSpec((tm, BRANCH_WIDTH), lambda i: (i + off, 1)),
        pl.BlockSpec((tm, BRANCH_WIDTH), lambda i: (i + off, 2)),
        pl.BlockSpec((1, HEAD_W), lambda i: (0, 0)),
        pl.BlockSpec((1, HEAD_W), lambda i: (0, 0)),
        pl.BlockSpec((HEAD_W, HEAD_W), lambda i: (0, 0)),
    ]
    args = [qkv, qkv, qkv, gq, gk, gm]
    if rope:
        tps = DEC_SEQ // tm
        in_specs += [pl.BlockSpec((tm, HEAD_W), lambda i: (lax.rem(i, tps), 0))] * 3
        args += list(tables)
    out_specs = [pl.BlockSpec((tm, BRANCH_WIDTH), lambda i: (i, 0))] * 2
    out_shape = [jax.ShapeDtypeStruct((nrows, BRANCH_WIDTH), BF16)] * 2
    if store:
        out_specs += [pl.BlockSpec((tm, BRANCH_WIDTH), lambda i: (i, 0))] * 2
        out_shape += [jax.ShapeDtypeStruct((nrows, BRANCH_WIDTH), F32)] * 2
    return pl.pallas_call(
        functools.partial(_attn_pre_kernel, rope=rope, store=store),
        grid=(nrows // tm,),
        in_specs=in_specs,
        out_specs=out_specs,
        out_shape=out_shape,
        compiler_params=_cparams(("arbitrary",)),
        name=name,
    )(*args)


def _attn_kernel(lam_ref, q_ref, *refs, lam_init, nkv):
    k_refs = refs[:nkv]
    v_refs = refs[nkv:2 * nkv]
    g_ref, o_ref = refs[2 * nkv:]
    lam = lam_ref[0]
    qb = q_ref[...]
    ks = [k_ref[...].astype(BF16) for k_ref in k_refs]
    es, coef = [], []
    for m in (0, 1):
        cols = slice(m * DA_HEADDIM, (m + 1) * DA_HEADDIM)
        ss = [lax.dot_general(qb[:, cols], kb[:, cols], (((1,), (1,)), ((), ())),
                              preferred_element_type=F32) for kb in ks]
        mx = functools.reduce(jnp.maximum, [jnp.max(s, axis=-1, keepdims=True) for s in ss])
        em = [jnp.exp2(s - mx) for s in ss]
        tot = functools.reduce(lambda a, b: a + b, [jnp.sum(e, axis=-1, keepdims=True) for e in em])
        es.append(em)
        coef.append(1.0 / tot)
    o = None
    for i in range(nkv):
        w = es[0][i] * coef[0] - es[1][i] * (lam * coef[1])
        t = jnp.dot(w.astype(BF16), v_refs[i][...].astype(BF16), preferred_element_type=F32)
        o = t if o is None else o + t
    o = o * lax.rsqrt(jnp.mean(o * o, axis=-1, keepdims=True) + NORM_EPS) * g_ref[...]
    o_ref[...] = (o * (1.0 - lam_init)).astype(o_ref.dtype)


def _attention(lam, q, ks, vs, kv_specs, subln_g, *, nseq, lq, lam_init, name):
    tq = min(lq, ATTN_TQ)
    nqb = lq // tq
    nkv = len(ks)
    grid_spec = pltpu.PrefetchScalarGridSpec(
        num_scalar_prefetch=0,
        grid=(nseq, DA_HEADS, nqb),
        in_specs=[
            pl.BlockSpec(memory_space=pltpu.SMEM),
            pl.BlockSpec((tq, HEAD_W), lambda b, h, i: (b * nqb + i, h)),
            *kv_specs,
            pl.BlockSpec((1, HEAD_W), lambda b, h, i: (0, 0)),
        ],
        out_specs=pl.BlockSpec((tq, HEAD_W), lambda b, h, i: (b * nqb + i, h)),
    )
    return pl.pallas_call(
        functools.partial(_attn_kernel, lam_init=lam_init, nkv=nkv),
        grid_spec=grid_spec,
        out_shape=jax.ShapeDtypeStruct((nseq * lq, DA_HEADS * HEAD_W), BF16),
        compiler_params=_cparams(("arbitrary", "arbitrary", "arbitrary")),
        name=name,
    )(lam, q, *ks, *vs, subln_g.reshape(1, HEAD_W))


def _sgu_kernel(u_ref, v_ref, g_ref, ws_ref, bt_ref, o_ref, *, chunks):
    v = v_ref[...].astype(F32)
    mu = jnp.mean(v, axis=-1, keepdims=True)
    vc = v - mu
    var = jnp.mean(vc * vc, axis=-1, keepdims=True)
    vn = (vc * lax.rsqrt(var + NORM_EPS) * g_ref[...]).astype(BF16)
    u = u_ref[...].astype(F32)
    gw = BRANCH_WIDTH // SG_GROUPS
    for c in range(chunks):
        rows = slice(c * SG_CHUNK, (c + 1) * SG_CHUNK)
        outs = []
        for g in range(SG_GROUPS):
            cols = slice(g * gw, (g + 1) * gw)
            mixed = jnp.dot(ws_ref[g].astype(BF16), vn[rows, cols], preferred_element_type=F32)
            mixed = mixed + bt_ref[:, g:g + 1]
            outs.append(u[rows, cols] * mixed)
        o_ref[rows, :] = jnp.concatenate(outs, axis=1).astype(o_ref.dtype)


def _sgu(u, v, g, ws, bs_t):
    chunks = 4
    tm = chunks * SG_CHUNK
    return pl.pallas_call(
        functools.partial(_sgu_kernel, chunks=chunks),
        grid=(N_TOK // tm,),
        in_specs=[
            pl.BlockSpec((tm, BRANCH_WIDTH), lambda i: (i, 0)),
            pl.BlockSpec((tm, BRANCH_WIDTH), lambda i: (i, 1)),
            pl.BlockSpec((1, BRANCH_WIDTH), lambda i: (0, 0)),
            pl.BlockSpec((SG_GROUPS, SG_CHUNK, SG_CHUNK), lambda i: (0, 0, 0)),
            pl.BlockSpec((SG_CHUNK, SG_GROUPS), lambda i: (0, 0)),
        ],
        out_specs=pl.BlockSpec((tm, BRANCH_WIDTH), lambda i: (i, 0)),
        out_shape=jax.ShapeDtypeStruct((N_TOK, BRANCH_WIDTH), BF16),
        compiler_params=_cparams(("arbitrary",)),
        name="sgu",
    )(u, v, g.reshape(1, BRANCH_WIDTH), ws, bs_t)


def _merge_kernel(ya_ref, yb_ref, yc_ref, wb_ref, ga_ref, gb_ref, gc_ref, o_ref):
    acc = None
    for k, (y_ref, g_ref) in enumerate(((ya_ref, ga_ref), (yb_ref, gb_ref), (yc_ref, gc_ref))):
        t = jnp.dot(y_ref[...], wb_ref[k], preferred_element_type=F32)
        t = t * jax.nn.sigmoid(g_ref[...].astype(F32))
        acc = t if acc is None else acc + t
    o_ref[...] = acc.astype(o_ref.dtype)


def _merge(ya, yb, yc, wb, gates):
    tm, tn = 1024, 1024
    nj = D_MODEL // tn
    yspec = pl.BlockSpec((tm, BRANCH_WIDTH), lambda i, j: (i, 0))
    return pl.pallas_call(
        _merge_kernel,
        grid=(N_TOK // tm, nj),
        in_specs=[
            yspec, yspec, yspec,
            pl.BlockSpec((3, BRANCH_WIDTH, tn), lambda i, j: (0, 0, j)),
            pl.BlockSpec((tm, tn), lambda i, j: (i, j)),
            pl.BlockSpec((tm, tn), lambda i, j: (i, nj + j)),
            pl.BlockSpec((tm, tn), lambda i, j: (i, 2 * nj + j)),
        ],
        out_specs=pl.BlockSpec((tm, tn), lambda i, j: (i, j)),
        out_shape=jax.ShapeDtypeStruct((N_TOK, D_MODEL), BF16),
        compiler_params=_cparams(("arbitrary", "arbitrary")),
        name="merge",
    )(ya, yb, yc, wb, gates, gates, gates)


HALF_D = D_MODEL // 2


def _pack_halves(hi_half, lo_half):
    top = lax.bitcast_convert_type(hi_half, jnp.uint32)
    bot = lax.shift_right_logical(lax.bitcast_convert_type(lo_half, jnp.uint32), jnp.uint32(16))
    return top | bot


def _pack_row_bf16(x):
    xb = x.astype(BF16).astype(F32)
    return _pack_halves(xb[:, :HALF_D], xb[:, HALF_D:])


def _unpack_row_bf16(p):
    a = lax.bitcast_convert_type(p & jnp.uint32(0xFFFF0000), F32)
    b = lax.bitcast_convert_type(lax.shift_left(p, jnp.uint32(16)), F32)
    return a, b


def _outproj_kernel(m_ref, w_ref, x_ref, g1_ref, n2_ref, sc_ref, sh_ref, wrh_ref, wrl_ref, br_ref,
                    x1_ref, h2_ref, lg_ref):
    out = jnp.dot(m_ref[...], w_ref[...], preferred_element_type=F32)
    x1 = x_ref[...] + g1_ref[0] * out
    x1_ref[...] = x1
    ms = jnp.mean(x1 * x1, axis=-1, keepdims=True)
    y = x1 * lax.rsqrt(ms + NORM_EPS) * n2_ref[...]
    h2 = y * (1.0 + sc_ref[0]) + sh_ref[0]
    hi = h2.astype(BF16)
    lo = (h2 - hi.astype(F32)).astype(BF16)
    h2_ref[...] = _pack_row_bf16(h2)
    wh = wrh_ref[...]
    lg = (jnp.dot(hi, wh, preferred_element_type=F32) + jnp.dot(lo, wh, preferred_element_type=F32)
          + jnp.dot(hi, wrl_ref[...], preferred_element_type=F32))
    lg_ref[...] = lg + br_ref[...]


def _outproj(merged, w_out, x, g1, n2, sc2, sh2, wr_hi, wr_lo, br):
    tm = 512
    cond = lambda i: (_cond_of_tile(i, tm), 0, 0)
    row = lambda i: (i, 0)
    fixed = lambda i: (0, 0)
    return pl.pallas_call(
        _outproj_kernel,
        grid=(N_TOK // tm,),
        in_specs=[
            pl.BlockSpec((tm, D_MODEL), row),
            pl.BlockSpec((D_MODEL, D_MODEL), fixed),
            pl.BlockSpec((tm, D_MODEL), row),
            pl.BlockSpec((1, 1, D_MODEL), cond),
            pl.BlockSpec((1, D_MODEL), fixed),
            pl.BlockSpec((1, 1, D_MODEL), cond),
            pl.BlockSpec((1, 1, D_MODEL), cond),
            pl.BlockSpec((D_MODEL, LANE), fixed),
            pl.BlockSpec((D_MODEL, LANE), fixed),
            pl.BlockSpec((1, LANE), fixed),
        ],
        out_specs=[
            pl.BlockSpec((tm, D_MODEL), row),
            pl.BlockSpec((tm, HALF_D), row),
            pl.BlockSpec((tm, LANE), row),
        ],
        out_shape=[
            jax.ShapeDtypeStruct((N_TOK, D_MODEL), F32),
            jax.ShapeDtypeStruct((N_TOK, HALF_D), jnp.uint32),
            jax.ShapeDtypeStruct((N_TOK, LANE), F32),
        ],
        compiler_params=_cparams(("arbitrary",)),
        name="outproj",
    )(merged, w_out, x, g1.reshape(N_COND, 1, D_MODEL), n2.reshape(1, D_MODEL),
      sc2.reshape(N_COND, 1, D_MODEL), sh2.reshape(N_COND, 1, D_MODEL), wr_hi, wr_lo, br)


def _used(b, nb):
    return jnp.minimum(b, nb[0] - 1)


MOE_HALF = MOE_BM // 2


def _moe_gu_kernel(be_ref, nb_ref, fr_ref, nv_ref, x_ref, wg_ref, wu_ref, bg_ref, bu_ref, o_ref, wg_bf, wu_bf):
    b = pl.program_id(1)
    live = b < nb_ref[0]

    @pl.when(jnp.logical_not(live))
    def _():
        o_ref[...] = jnp.zeros_like(o_ref)

    @pl.when(jnp.logical_and(live, fr_ref[b] == 1))
    def _():
        wg_bf[...] = wg_ref[0, 0].astype(BF16)
        wu_bf[...] = wu_ref[0, 0].astype(BF16)

    def half(r0):
        rows = pl.ds(r0, MOE_HALF)
        xa, xb = _unpack_row_bf16(x_ref[rows, :])
        xa = xa.astype(BF16)
        xb = xb.astype(BF16)
        gate = (jnp.dot(xa, wg_bf[:HALF_D, :], preferred_element_type=F32)
                + jnp.dot(xb, wg_bf[HALF_D:, :], preferred_element_type=F32) + bg_ref[0, 0])
        up = (jnp.dot(xa, wu_bf[:HALF_D, :], preferred_element_type=F32)
              + jnp.dot(xb, wu_bf[HALF_D:, :], preferred_element_type=F32) + bu_ref[0, 0])
        gate = jnp.minimum(gate, SWIGLU_LIMIT)
        up = jnp.clip(up, -SWIGLU_LIMIT, SWIGLU_LIMIT)
        act = (up + 1.0) * gate * jax.nn.sigmoid(SWIGLU_ALPHA * gate)
        o_ref[rows, :] = act.astype(o_ref.dtype)

    @pl.when(live)
    def _():
        half(0)

    @pl.when(jnp.logical_and(live, nv_ref[b] > MOE_HALF))
    def _():
        half(MOE_HALF)

    @pl.when(jnp.logical_and(live, nv_ref[b] <= MOE_HALF))
    def _():
        o_ref[pl.ds(MOE_HALF, MOE_HALF), :] = jnp.zeros((MOE_HALF, o_ref.shape[1]), o_ref.dtype)


def _moe_gu(blk_expert, n_used, blk_first, blk_rows, x_rows, w_gu, b_gu, layer):
    tn = 1024
    nj = D_EXPERT // tn
    grid_spec = pltpu.PrefetchScalarGridSpec(
        num_scalar_prefetch=4,
        grid=(nj, MOE_BLOCKS),
        in_specs=[
            pl.BlockSpec((MOE_BM, HALF_D), lambda j, b, be, nb, fr, nv: (_used(b, nb), 0)),
            pl.BlockSpec((1, 1, D_MODEL, tn), lambda j, b, be, nb, fr, nv: (layer, be[_used(b, nb)], 0, j)),
            pl.BlockSpec((1, 1, D_MODEL, tn), lambda j, b, be, nb, fr, nv: (layer, be[_used(b, nb)], 0, nj + j)),
            pl.BlockSpec((1, 1, 1, tn), lambda j, b, be, nb, fr, nv: (layer, be[_used(b, nb)], 0, j)),
            pl.BlockSpec((1, 1, 1, tn), lambda j, b, be, nb, fr, nv: (layer, be[_used(b, nb)], 0, nj + j)),
        ],
        out_specs=pl.BlockSpec((MOE_BM, tn), lambda j, b, be, nb, fr, nv: (b, j)),
        scratch_shapes=[pltpu.VMEM((D_MODEL, tn), BF16), pltpu.VMEM((D_MODEL, tn), BF16)],
    )
    b4 = b_gu.reshape(DEPTH, N_EXPERTS, 1, 2 * D_EXPERT)
    return pl.pallas_call(
        _moe_gu_kernel,
        grid_spec=grid_spec,
        out_shape=jax.ShapeDtypeStruct((MOE_ROWS, D_EXPERT), BF16),
        compiler_params=_cparams(("arbitrary", "arbitrary")),
        name="moe_gu",
    )(blk_expert, n_used, blk_first, blk_rows, x_rows, w_gu, w_gu, b4, b4)


def _moe_down_kernel(be_ref, nb_ref, fr_ref, nv_ref, a_ref, w_ref, b_ref, o_ref, w_bf):
    b = pl.program_id(0)
    live = b < nb_ref[0]

    @pl.when(jnp.logical_not(live))
    def _():
        o_ref[...] = jnp.zeros_like(o_ref)

    @pl.when(jnp.logical_and(live, fr_ref[b] == 1))
    def _():
        w_bf[...] = w_ref[0, 0].astype(BF16)

    def half(r0):
        rows = pl.ds(r0, MOE_HALF)
        y = jnp.dot(a_ref[rows, :], w_bf[...], preferred_element_type=F32) + b_ref[0, 0]
        o_ref[rows, :] = _pack_row_bf16(y)

    @pl.when(live)
    def _():
        half(0)

    @pl.when(jnp.logical_and(live, nv_ref[b] > MOE_HALF))
    def _():
        half(MOE_HALF)

    @pl.when(jnp.logical_and(live, nv_ref[b] <= MOE_HALF))
    def _():
        o_ref[pl.ds(MOE_HALF, MOE_HALF), :] = jnp.zeros((MOE_HALF, o_ref.shape[1]), o_ref.dtype)


def _moe_down(blk_expert, n_used, blk_first, blk_rows, act, w_down, b_down, layer):
    grid_spec = pltpu.PrefetchScalarGridSpec(
        num_scalar_prefetch=4,
        grid=(MOE_BLOCKS,),
        in_specs=[
            pl.BlockSpec((MOE_BM, D_EXPERT), lambda b, be, nb, fr, nv: (_used(b, nb), 0)),
            pl.BlockSpec((1, 1, D_EXPERT, D_MODEL), lambda b, be, nb, fr, nv: (layer, be[_used(b, nb)], 0, 0)),
            pl.BlockSpec((1, 1, 1, D_MODEL), lambda b, be, nb, fr, nv: (layer, be[_used(b, nb)], 0, 0)),
        ],
        out_specs=pl.BlockSpec((MOE_BM, HALF_D), lambda b, be, nb, fr, nv: (b, 0)),
        scratch_shapes=[pltpu.VMEM((D_EXPERT, D_MODEL), BF16)],
    )
    return pl.pallas_call(
        _moe_down_kernel,
        grid_spec=grid_spec,
        out_shape=jax.ShapeDtypeStruct((MOE_ROWS, HALF_D), jnp.uint32),
        compiler_params=_cparams(("arbitrary",)),
        name="moe_down",
    )(blk_expert, n_used, blk_first, blk_rows, act, w_down, b_down.reshape(DEPTH, N_EXPERTS, 1, D_MODEL))


ROUTE_TM = 512


def _route_kernel(lg_ref, ei_ref, ew_ref, cnt_ref, base):
    i = pl.program_id(0)

    @pl.when(i == 0)
    def _():
        base[...] = jnp.zeros_like(base)

    tm = ROUTE_TM
    lg = lg_ref[...]
    lane = lax.broadcasted_iota(jnp.int32, (tm, LANE), 1)
    vals, idxs = [], []
    for _ in range(TOP_K):
        m = jnp.max(lg, axis=-1, keepdims=True)
        idx = jnp.min(jnp.where(lg == m, lane, LANE), axis=-1, keepdims=True)
        vals.append(m)
        idxs.append(idx)
        lg = jnp.where(lane == idx, -jnp.inf, lg)
    es = [jnp.exp(v - vals[0]) for v in vals]
    inv = 1.0 / functools.reduce(lambda a, b: a + b, es)
    onehot = functools.reduce(lambda a, b: a + b, [jnp.where(lane == idx, 1.0, 0.0) for idx in idxs])
    ri = lax.broadcasted_iota(jnp.int32, (tm, tm), 0)
    ci = lax.broadcasted_iota(jnp.int32, (tm, tm), 1)
    strict_lower = jnp.where(ri > ci, 1.0, 0.0).astype(BF16)
    before = jnp.dot(strict_lower, onehot.astype(BF16), preferred_element_type=F32) + base[...]
    ei = jnp.zeros((tm, LANE), jnp.int32)
    ew = jnp.zeros((tm, LANE), F32)
    for k in range(TOP_K):
        rank = jnp.sum(jnp.where(lane == idxs[k], before, 0.0), axis=-1, keepdims=True)
        ei = jnp.where(lane == k, idxs[k], ei)
        ei = jnp.where(lane == TOP_K + k, rank.astype(jnp.int32), ei)
        ew = jnp.where(lane == k, es[k] * inv, ew)
    ei_ref[...] = ei
    ew_ref[...] = ew
    base[...] = base[...] + jnp.sum(onehot, axis=0, keepdims=True)
    cnt_ref[...] = base[...]


def _route(logits):
    tm = ROUTE_TM
    return pl.pallas_call(
        _route_kernel,
        grid=(N_TOK // tm,),
        in_specs=[pl.BlockSpec((tm, LANE), lambda i: (i, 0))],
        out_specs=[
            pl.BlockSpec((tm, LANE), lambda i: (i, 0)),
            pl.BlockSpec((tm, LANE), lambda i: (i, 0)),
            pl.BlockSpec((1, LANE), lambda i: (0, 0)),
        ],
        out_shape=[
            jax.ShapeDtypeStruct((N_TOK, LANE), jnp.int32),
            jax.ShapeDtypeStruct((N_TOK, LANE), F32),
            jax.ShapeDtypeStruct((1, LANE), F32),
        ],
        scratch_shapes=[pltpu.VMEM((1, LANE), F32)],
        compiler_params=_cparams(("arbitrary",)),
        name="route",
    )(logits)


def _route_layout(ei, counts):
    cnt = counts[0, :N_EXPERTS].astype(jnp.int32)
    padded = (cnt + MOE_BM - 1) // MOE_BM * MOE_BM
    pad_end = jnp.cumsum(padded)
    pad_start = pad_end - padded
    dest = pad_start[ei[:, :TOP_K]] + ei[:, TOP_K:2 * TOP_K]
    blk_row0 = jnp.arange(MOE_BLOCKS, dtype=jnp.int32) * MOE_BM
    blk_expert = jnp.minimum(jnp.sum((pad_end[None, :] <= blk_row0[:, None]).astype(jnp.int32), axis=1),
                             N_EXPERTS - 1)
    n_used = (pad_end[-1:] // MOE_BM).astype(jnp.int32)
    blk_first = jnp.concatenate([jnp.ones((1,), jnp.int32),
                                 (blk_expert[1:] != blk_expert[:-1]).astype(jnp.int32)])
    blk_rows = jnp.clip(pad_start[blk_expert] + cnt[blk_expert] - blk_row0, 0, MOE_BM).astype(jnp.int32)
    fill_start = (pad_start + cnt).astype(jnp.int32)
    fill_cnt = (padded - cnt).astype(jnp.int32)
    dest3 = dest.astype(jnp.int32).reshape(N_TOK // MOE_TT, 1, MOE_TT * TOP_K)
    return dest3, blk_expert, n_used, blk_first, blk_rows, fill_start, fill_cnt


MOE_TT = 256


def _row_copy(src, dst, sem):
    return pltpu.make_async_copy(src, dst, sem)


def _scatter_kernel(fs_ref, fc_ref, nb_ref, dest_ref, h_ref, x_hbm, zblk, sem, zsem):
    i = pl.program_id(0)

    @pl.when(i == 0)
    def _():
        zblk[...] = jnp.zeros_like(zblk)
        zrow = zblk.at[pl.ds(0, 1)]

        def fill_expert(e, carry):
            def fill_row(r, c):
                _row_copy(zrow, x_hbm.at[pl.ds(fs_ref[e] + r, 1)], zsem).start()
                return c
            lax.fori_loop(0, fc_ref[e], fill_row, 0)

            def wait_row(r, c):
                _row_copy(zrow, x_hbm.at[pl.ds(0, 1)], zsem).wait()
                return c
            lax.fori_loop(0, fc_ref[e], wait_row, 0)
            return carry
        lax.fori_loop(0, N_EXPERTS, fill_expert, 0)

        def fill_block(b, carry):
            cp = _row_copy(zblk, x_hbm.at[pl.ds(pl.multiple_of(b * MOE_BM, MOE_BM), MOE_BM)], zsem)
            cp.start()
            cp.wait()
            return carry
        lax.fori_loop(nb_ref[0], MOE_BLOCKS, fill_block, 0)

    def issue(t, carry):
        for k in range(TOP_K):
            d = dest_ref[0, 0, t * TOP_K + k]
            _row_copy(h_ref.at[pl.ds(t, 1)], x_hbm.at[pl.ds(d, 1)], sem).start(priority=k % 2)
        return carry
    lax.fori_loop(0, MOE_TT, issue, 0, unroll=8)

    def drain(t, carry):
        _row_copy(h_ref.at[pl.ds(0, 1)], x_hbm.at[pl.ds(0, 1)], sem).wait()
        return carry
    lax.fori_loop(0, MOE_TT * TOP_K, drain, 0, unroll=8)


def _scatter_rows(fill_start, fill_cnt, n_used, dest3, h2):
    grid_spec = pltpu.PrefetchScalarGridSpec(
        num_scalar_prefetch=3,
        grid=(N_TOK // MOE_TT,),
        in_specs=[
            pl.BlockSpec((1, 1, MOE_TT * TOP_K), lambda i, fs, fc, nb: (i, 0, 0), memory_space=pltpu.SMEM),
            pl.BlockSpec((MOE_TT, HALF_D), lambda i, fs, fc, nb: (i, 0)),
        ],
        out_specs=pl.BlockSpec(memory_space=pl.ANY),
        scratch_shapes=[
            pltpu.VMEM((MOE_BM, HALF_D), jnp.uint32),
            pltpu.SemaphoreType.DMA,
            pltpu.SemaphoreType.DMA,
        ],
    )
    return pl.pallas_call(
        _scatter_kernel,
        grid_spec=grid_spec,
        out_shape=jax.ShapeDtypeStruct((MOE_ROWS, HALF_D), jnp.uint32),
        compiler_params=_cparams(("arbitrary",)),
        name="moe_scatter",
    )(fill_start, fill_cnt, n_used, dest3, h2)


def _combine_kernel(dcur_ref, dnext_ref, w_ref, x1_ref, g2_ref, y_hbm, o_ref, buf, sem):
    i = pl.program_id(0)
    n = pl.num_programs(0)
    slot = lax.rem(i, 2)

    def issue(d_ref, s):
        def body(t, carry):
            for k in range(TOP_K):
                d = d_ref[0, 0, t * TOP_K + k]
                _row_copy(y_hbm.at[pl.ds(d, 1)], buf.at[s, k, pl.ds(t, 1)], sem.at[s]).start(priority=k % 2)
            return carry
        lax.fori_loop(0, MOE_TT, body, 0, unroll=8)

    @pl.when(i == 0)
    def _():
        issue(dcur_ref, 0)

    @pl.when(i + 1 < n)
    def _():
        issue(dnext_ref, 1 - slot)

    def drain(t, carry):
        _row_copy(y_hbm.at[pl.ds(0, 1)], buf.at[slot, 0, pl.ds(0, 1)], sem.at[slot]).wait()
        return carry
    lax.fori_loop(0, MOE_TT * TOP_K, drain, 0, unroll=8)

    acc_a = acc_b = None
    for k in range(TOP_K):
        ya, yb = _unpack_row_bf16(buf[slot, k])
        wk = w_ref[:, k:k + 1]
        acc_a = ya * wk if acc_a is None else acc_a + ya * wk
        acc_b = yb * wk if acc_b is None else acc_b + yb * wk
    o_ref[:, :HALF_D] = x1_ref[:, :HALF_D] + g2_ref[0, :, :HALF_D] * acc_a
    o_ref[:, HALF_D:] = x1_ref[:, HALF_D:] + g2_ref[0, :, HALF_D:] * acc_b


def _combine(dest3, ew, x1, g2, y_rows):
    nt = N_TOK // MOE_TT
    return pl.pallas_call(
        _combine_kernel,
        grid=(nt,),
        in_specs=[
            pl.BlockSpec((1, 1, MOE_TT * TOP_K), lambda i: (i, 0, 0), memory_space=pltpu.SMEM),
            pl.BlockSpec((1, 1, MOE_TT * TOP_K), lambda i: (jnp.minimum(i + 1, nt - 1), 0, 0),
                         memory_space=pltpu.SMEM),
            pl.BlockSpec((MOE_TT, LANE), lambda i: (i, 0)),
            pl.BlockSpec((MOE_TT, D_MODEL), lambda i: (i, 0)),
            pl.BlockSpec((1, 1, D_MODEL), lambda i: (_cond_of_tile(i, MOE_TT), 0, 0)),
            pl.BlockSpec(memory_space=pl.ANY),
        ],
        out_specs=pl.BlockSpec((MOE_TT, D_MODEL), lambda i: (i, 0)),
        out_shape=jax.ShapeDtypeStruct((N_TOK, D_MODEL), F32),
        scratch_shapes=[
            pltpu.VMEM((2, TOP_K, MOE_TT, HALF_D), jnp.uint32),
            pltpu.SemaphoreType.DMA((2,)),
        ],
        compiler_params=_cparams(("arbitrary",)),
        name="moe_combine",
    )(dest3, dest3, ew, x1, g2.reshape(N_COND, 1, D_MODEL), y_rows)


def _rope_tables(n):
    t = jnp.arange(n)
    row = (t // GRID_W).astype(F32)
    col = (t % GRID_W).astype(F32)
    quarter = DA_HEADDIM // 4
    inv_freq = ROPE_BASE ** (-jnp.arange(quarter, dtype=F32) / quarter)
    ang_r = row[:, None] * inv_freq
    ang_c = col[:, None] * inv_freq
    ang = jnp.concatenate([ang_r, ang_r, ang_c, ang_c] * 2, axis=-1)
    cos, sin = jnp.cos(ang), jnp.sin(ang)
    lower = (jnp.arange(HEAD_W) % (2 * quarter)) < quarter
    return cos, jnp.where(lower, -sin, 0.0), jnp.where(lower, 0.0, sin)


def _group_mean_matrix():
    g = jnp.arange(HEAD_W) // DA_HEADDIM
    return jnp.where(g[:, None] == g[None, :], 1.0 / DA_HEADDIM, 0.0).astype(BF16)


def _layer(l, x, mod, p, cache_k, cache_v, h0_t):
    sh1, sc1, g1, sh2, sc2, g2 = [mod[l, :N_COND, i * D_MODEL:(i + 1) * D_MODEL] for i in range(6)]
    w_in = p['w_in'][l]
    o_dt = 2 * BRANCH_WIDTH + 2 * SSD_BC
    o_q = o_dt + 2 * SSD_HEADS
    o_u = o_q + 3 * BRANCH_WIDTH
    o_g = o_u + 2 * BRANCH_WIDTH
    wdt = jnp.pad(w_in[:, o_dt:o_q], ((0, 0), (0, LANE - 2 * SSD_HEADS))).astype(BF16)
    h, dt_raw = _prologue(x, p['norm1_g'][l], sc1, sh1, wdt)

    o_x = BRANCH_WIDTH
    z = _matmul(h, w_in[:, :o_x].astype(BF16), 1024, 1024, BF16, "inproj_z")
    xbc = _matmul(h, w_in[:, o_x:o_dt].astype(BF16), 1024, 1280, BF16, "inproj_xbc")
    qkv = _matmul(h, w_in[:, o_q:o_u].astype(BF16), 1024, 1024, BF16, "inproj_attn")
    uv = _matmul(h, w_in[:, o_u:o_g].astype(BF16), 1024, 1024, BF16, "inproj_sgu")
    gates = _matmul(h, w_in[:, o_g:].astype(BF16), 1024, 1024, BF16, "inproj_gates")

    xc = _conv_silu(xbc, p['conv_w'][l], p['conv_b'][l])
    dt_bias = jnp.pad(p['dt_bias'][l].reshape(1, 2 * SSD_HEADS), ((0, 0), (0, LANE - 2 * SSD_HEADS)))
    a_neg = jnp.pad(-jnp.exp(p['a_log'][l].reshape(1, 2 * SSD_HEADS)), ((0, 0), (0, LANE - 2 * SSD_HEADS)))
    y_f, y_b, h_fin = _ssd(xc, dt_raw, dt_bias, a_neg, h0_t[:, l])
    y_a = _ssd_post(y_f, y_b, xc, z, jnp.repeat(p['d_skip'][l], SSD_HEADDIM), p['ssm_norm_g'][l])
    state_l = jnp.swapaxes(h_fin[:BATCH], -1, -2)

    gq = jnp.tile(p['qk_norm_g'][l][0], 2).reshape(1, HEAD_W) * (DA_HEADDIM ** -0.5 * math.log2(math.e))
    gk = jnp.tile(p['qk_norm_g'][l][1], 2).reshape(1, HEAD_W)
    gm = _group_mean_matrix()
    lam_init = 0.8 - 0.6 * math.exp(-0.3 * l)
    lv = p['lam'][l]
    lam = (jnp.exp(jnp.sum(lv[0] * lv[1])) - jnp.exp(jnp.sum(lv[2] * lv[3])) + lam_init).reshape(1)
    qp, kp, k_store, v_store = _attn_pre(qkv, gq, gk, gm, None, row_off=0, nrows=N_PROMPT, store=True,
                                         name="attn_pre_prompt")
    kv_p = [pl.BlockSpec((SEQ, HEAD_W), lambda b, h, i: (b, h)),
            pl.BlockSpec((SEQ, HEAD_W), lambda b, h, i: (b, QKV_V_BLOCK + h))]
    o_p = _attention(lam, qp, [kp], [qkv], kv_p, p['subln_g'][l], nseq=BATCH, lq=SEQ,
                     lam_init=lam_init, name="attn_prompt")
    qs, ks = _attn_pre(qkv, gq, gk, gm, _rope_tables(DEC_SEQ), row_off=N_PROMPT, nrows=N_SAMPLE, store=False,
                       name="attn_pre_sample")
    seq0 = N_PROMPT // DEC_SEQ
    ctx_spec = pl.BlockSpec((None, None, PAST_LEN, HEAD_W), lambda b, h, i: (b, l, 0, h))
    kv_s = [pl.BlockSpec((DEC_SEQ, HEAD_W), lambda b, h, i: (b, h)), ctx_spec,
            pl.BlockSpec((DEC_SEQ, HEAD_W), lambda b, h, i: (seq0 + b, QKV_V_BLOCK + h)), ctx_spec]
    ck = cache_k.reshape(DEC_BATCH, DEPTH, PAST_LEN, BRANCH_WIDTH)
    cv = cache_v.reshape(DEC_BATCH, DEPTH, PAST_LEN, BRANCH_WIDTH)
    o_s = _attention(lam, qs, [ks, ck], [qkv, cv], kv_s, p['subln_g'][l], nseq=DEC_BATCH, lq=DEC_SEQ,
                     lam_init=lam_init, name="attn_sample")
    y_b2 = jnp.concatenate([o_p, o_s], axis=0)
    k_store = k_store.reshape(BATCH, SEQ, DA_HEADS, HEAD_W)
    v_store = v_store.reshape(BATCH, SEQ, DA_HEADS, HEAD_W)

    y_c = _sgu(uv, uv, p['sgu_norm_g'][l], p['w_s'][l], p['b_s'][l].T)

    merged = _merge(y_a, y_b2, y_c, p['w_branch'][l].astype(BF16), gates)
    wr = jnp.pad(p['w_router'][l], ((0, 0), (0, LANE - N_EXPERTS)))
    wr_hi = wr.astype(BF16)
    wr_lo = (wr - wr_hi.astype(F32)).astype(BF16)
    br = jnp.pad(p['b_router'][l].reshape(1, N_EXPERTS), ((0, 0), (0, LANE - N_EXPERTS)),
                 constant_values=-1e30)
    x1, h2, logits = _outproj(merged, p['w_out'][l].astype(BF16), x, g1, p['norm2_g'][l], sc2, sh2,
                              wr_hi, wr_lo, br)

    ei, ew, counts = _route(logits)
    dest3, blk_expert, n_used, blk_first, blk_rows, fill_start, fill_cnt = _route_layout(ei, counts)
    x_rows = _scatter_rows(fill_start, fill_cnt, n_used, dest3, h2)
    act = _moe_gu(blk_expert, n_used, blk_first, blk_rows, x_rows, p['w_gu'], p['b_gu'], l)
    y_rows = _moe_down(blk_expert, n_used, blk_first, blk_rows, act, p['w_down'], p['b_down'], l)
    x2 = _combine(dest3, ew, x1, g2, y_rows)
    return x2, k_store, v_store, state_l


def kernel(x_prompt, x_sample, c, cache_k, cache_v, state_ssm, c_ctx, w_ada, b_ada, norm1_g, norm2_g,
           w_in, conv_w, conv_b, dt_bias, a_log, d_skip, ssm_norm_g, qk_norm_g, lam, subln_g,
           sgu_norm_g, w_s, b_s, w_branch, w_out, w_router, b_router, w_gu, b_gu, w_down, b_down):
    p = {'norm1_g': norm1_g, 'norm2_g': norm2_g, 'w_in': w_in, 'conv_w': conv_w, 'conv_b': conv_b,
         'dt_bias': dt_bias, 'a_log': a_log, 'd_skip': d_skip, 'ssm_norm_g': ssm_norm_g,
         'qk_norm_g': qk_norm_g, 'lam': lam, 'subln_g': subln_g, 'sgu_norm_g': sgu_norm_g, 'w_s': w_s,
         'b_s': b_s, 'w_branch': w_branch, 'w_out': w_out, 'w_router': w_router, 'b_router': b_router,
         'w_gu': w_gu, 'b_gu': b_gu, 'w_down': w_down, 'b_down': b_down}
    cond8 = jnp.concatenate([c_ctx[None, :], c, jnp.zeros((SUBLANE - N_COND, D_MODEL), F32)], axis=0)
    mod = _ada_all(cond8, w_ada, b_ada)
    h0_t = jnp.concatenate([jnp.zeros((1,) + state_ssm.shape[1:], F32), state_ssm], axis=0)
    h0_t = jnp.swapaxes(h0_t, -1, -2)
    x = jnp.concatenate([x_prompt.reshape(N_PROMPT, D_MODEL), x_sample.reshape(N_SAMPLE, D_MODEL)], axis=0)
    ks, vs, ss = [], [], []
    for l in range(DEPTH):
        x, k_l, v_l, s_l = _layer(l, x, mod, p, cache_k, cache_v, h0_t)
        ks.append(k_l)
        vs.append(v_l)
        ss.append(s_l)
    y_prompt = x[:N_PROMPT].reshape(BATCH, SEQ, D_MODEL)
    y_sample = x[N_PROMPT:].reshape(DEC_BATCH, DEC_SEQ, D_MODEL)
    return (y_prompt, y_sample, jnp.stack(ks, axis=1), jnp.stack(vs, axis=1), jnp.stack(ss, axis=1))
```
